```python
import jax, jax.numpy as jnp
from jax import lax
import numpy as np

D_MODEL = 1024
BATCH = 16
SEQ = 2048
DEPTH = 4

GRID_W = 64
CTX_LEN = 256
HEAD_DIM = 64
D_RWKV = D_MODEL // 2
D_NA = D_MODEL - D_RWKV
H_RWKV = D_RWKV // HEAD_DIM
H_NA = D_NA // HEAD_DIM
NA_ROWS_MAX = 8
NA_COLS = 16
R_DECAY = 64
R_AAA = 64
R_GATE = 128
D_SHIFT = 3 * D_RWKV + 2 * R_DECAY + 2 * R_AAA + R_GATE
D_IN = 3 * D_NA + D_SHIFT
D_FF = 2816
N_EXPERTS = 8
TOP_K = 2
D_FF_EXPERT = 3584
MOE_BLOCK = 256
RMS_EPS = 1e-6
GN_EPS = 64e-5
L2_EPS = 1e-12

kernel_name = "hybrid_rwkv7_natten_moe_dit"


def _rms_norm(x, g):
    xf = x.astype(jnp.float32)
    y = xf * lax.rsqrt(jnp.mean(xf * xf, axis=-1, keepdims=True) + RMS_EPS)
    return (y * g.astype(jnp.float32)).astype(x.dtype)


def _centred_shift(p, mu):
    z = jnp.zeros_like(p[:, :1])
    prev = jnp.concatenate([z, p[:, :-1]], axis=1)
    nxt = jnp.concatenate([p[:, 1:], z], axis=1)
    return p + mu * (0.5 * (prev + nxt) - p)


def _rwkv_features(u, w0, w2, a0, a2, g2, k_k, k_a):
    B, T, _ = u.shape
    C = D_RWKV
    r = u[..., :C]
    k = u[..., C:2 * C]
    v = u[..., 2 * C:3 * C]
    o = 3 * C
    wl = u[..., o:o + 2 * R_DECAY].reshape(B, T, 2, R_DECAY)
    o += 2 * R_DECAY
    al = u[..., o:o + 2 * R_AAA].reshape(B, T, 2, R_AAA)
    o += 2 * R_AAA
    gl = u[..., o:o + R_GATE]
    w_log = -jax.nn.softplus(-(w0 + jnp.einsum('btzr,zrc->btzc', jnp.tanh(wl), w2))) - 0.5
    decay = jnp.exp(-jnp.exp(w_log.astype(jnp.float32)))
    a = jax.nn.sigmoid(a0 + jnp.einsum('btzr,zrc->btzc', al, a2))
    g = jax.nn.sigmoid(gl) @ g2
    kk = (k * k_k).reshape(B, T, H_RWKV, HEAD_DIM).astype(jnp.float32)
    kk = kk / jnp.maximum(jnp.sqrt(jnp.sum(kk * kk, axis=-1, keepdims=True)), L2_EPS)
    k_dir = k[:, :, None, :] * (1.0 + (a - 1.0) * k_a)
    return r, k_dir, v, kk, decay, a, g


def _rwkv_scan(feats, d, s0, reverse):
    r, k_dir, v, kk, decay, a, g = feats
    B, T, _ = r.shape

    def hs(t):
        return t.reshape(B, T, H_RWKV, HEAD_DIM).astype(jnp.float32)

    seqs = (hs(r), hs(decay[:, :, d]), hs(k_dir[:, :, d]), hs(v), kk, kk * hs(a[:, :, d]))
    xs = tuple(jnp.moveaxis(t, 1, 0) for t in seqs)

    def step(s, inp):
        r_t, w_t, k_t, v_t, kk_t, b_t = inp
        sa = jnp.einsum('bhvk,bhk->bhv', s, kk_t)
        s = s * w_t[:, :, None, :] - sa[..., None] * b_t[:, :, None, :] + v_t[..., None] * k_t[:, :, None, :]
        return s, jnp.einsum('bhvk,bhk->bhv', s, r_t)

    s_fin, ys = lax.scan(step, s0, xs, reverse=reverse)
    return jnp.moveaxis(ys, 0, 1), s_fin


def _rwkv_readout(y, feats, r_k, ln_w, ln_b):
    r, k_dir, v, kk, decay, a, g = feats
    B, T, _ = r.shape
    mu = jnp.mean(y, axis=-1, keepdims=True)
    var = jnp.mean(jnp.square(y - mu), axis=-1, keepdims=True)
    yn = ((y - mu) * lax.rsqrt(var + GN_EPS)).reshape(B, T, D_RWKV) * ln_w + ln_b
    rh = r.reshape(B, T, H_RWKV, HEAD_DIM)
    kh = jnp.mean(k_dir, axis=2).reshape(B, T, H_RWKV, HEAD_DIM)
    vh = v.reshape(B, T, H_RWKV, HEAD_DIM)
    bonus = (jnp.sum(rh * kh * r_k, axis=-1, keepdims=True) * vh).reshape(B, T, D_RWKV)
    return (yn.astype(r.dtype) + bonus) * g


def _rwkv_group(u_lat, u_ctx, w0, w2, a0, a2, g2, k_k, k_a, r_k, ln_w, ln_b, need_ctx_out):
    f_lat = _rwkv_features(u_lat, w0, w2, a0, a2, g2, k_k, k_a)
    f_ctx = _rwkv_features(u_ctx, w0, w2, a0, a2, g2, k_k, k_a)
    B = u_lat.shape[0]
    ys_lat, ys_ctx = [], []
    for d, rev in enumerate((False, True)):
        s0 = jnp.zeros((B, H_RWKV, HEAD_DIM, HEAD_DIM), jnp.float32)
        y_c, s_c = _rwkv_scan(f_ctx, d, s0, rev)
        y_l, _ = _rwkv_scan(f_lat, d, s_c, rev)
        ys_lat.append(y_l)
        ys_ctx.append(y_c)
    out_lat = _rwkv_readout(ys_lat[0] + ys_lat[1], f_lat, r_k, ln_w, ln_b)
    out_ctx = _rwkv_readout(ys_ctx[0] + ys_ctx[1], f_ctx, r_k, ln_w, ln_b) if need_ctx_out else None
    return out_lat, out_ctx


def _neighbourhood_attention(q, k, v, qc, kc, vc, rpb, need_ctx_out):
    B, T = q.shape[0], q.shape[1]
    rows = T // GRID_W
    kh = min(NA_ROWS_MAX, rows)
    kw = NA_COLS
    scale = HEAD_DIM ** -0.5
    qg = q.reshape(B, rows, GRID_W, H_NA, HEAD_DIM)
    kg = k.reshape(B, rows, GRID_W, H_NA, HEAD_DIM)
    vg = v.reshape(B, rows, GRID_W, H_NA, HEAD_DIM)
    cols = np.arange(GRID_W)
    col_idx = np.clip(cols - kw // 2, 0, GRID_W - kw)[:, None] + np.arange(kw)[None, :]
    col_off = col_idx - cols[:, None] + (NA_COLS - 1)
    rpb_cols = rpb[:, :, col_off]

    def one_row(r):
        r0 = jnp.clip(r - kh // 2, 0, rows - kh)
        q_row = lax.dynamic_index_in_dim(qg, r, axis=1, keepdims=False)
        k_win = lax.dynamic_slice_in_dim(kg, r0, kh, axis=1)[:, :, col_idx]
        v_win = lax.dynamic_slice_in_dim(vg, r0, kh, axis=1)[:, :, col_idx]
        row_off = r0 + jnp.arange(kh) - r + (NA_ROWS_MAX - 1)
        bias = jnp.take(rpb_cols, row_off, axis=1).transpose(0, 2, 1, 3)
        s_loc = jnp.einsum('bqhd,biqjhd->bhqij', q_row, k_win).astype(jnp.float32) * scale + bias.astype(jnp.float32)
        s_ctx = jnp.einsum('bqhd,blhd->bhql', q_row, kc).astype(jnp.float32) * scale
        s = jnp.concatenate([s_loc.reshape(B, H_NA, GRID_W, kh * kw), s_ctx], axis=-1)
        p = jax.nn.softmax(s, axis=-1).astype(v.dtype)
        p_loc = p[..., :kh * kw].reshape(B, H_NA, GRID_W, kh, kw)
        p_ctx = p[..., kh * kw:]
        return jnp.einsum('bhqij,biqjhd->bqhd', p_loc, v_win) + jnp.einsum('bhql,blhd->bqhd', p_ctx, vc)

    y = lax.map(one_row, jnp.arange(rows))
    y_lat = jnp.moveaxis(y, 0, 1).reshape(B, T, D_NA)
    y_ctx = None
    if need_ctx_out:
        s = jnp.einsum('blhd,bmhd->bhlm', qc, kc).astype(jnp.float32) * scale
        p = jax.nn.softmax(s, axis=-1).astype(vc.dtype)
        y_ctx = jnp.einsum('bhlm,bmhd->blhd', p, vc).reshape(B, qc.shape[1], D_NA)
    return y_lat, y_ctx


def _mixer(h_lat, h_ctx, w_in, shift_mu, w0, w2, a0, a2, g2, k_k, k_a, r_k, ln_w, ln_b, rpb, w_out, need_ctx_out):
    p_lat = h_lat @ w_in
    p_ctx = h_ctx @ w_in
    n_na = 3 * D_NA

    def heads(t):
        return t.reshape(t.shape[0], t.shape[1], H_NA, HEAD_DIM)

    q, k, v = [heads(t) for t in jnp.split(p_lat[..., :n_na], 3, axis=-1)]
    qc, kc, vc = [heads(t) for t in jnp.split(p_ctx[..., :n_na], 3, axis=-1)]
    y_na_lat, y_na_ctx = _neighbourhood_attention(q, k, v, qc, kc, vc, rpb, need_ctx_out)
    u_lat = _centred_shift(p_lat[..., n_na:], shift_mu)
    u_ctx = _centred_shift(p_ctx[..., n_na:], shift_mu)
    y_rw_lat, y_rw_ctx = _rwkv_group(u_lat, u_ctx, w0, w2, a0, a2, g2, k_k, k_a, r_k, ln_w, ln_b, need_ctx_out)
    out_lat = jnp.concatenate([y_rw_lat, y_na_lat], axis=-1) @ w_out
    out_ctx = jnp.concatenate([y_rw_ctx, y_na_ctx], axis=-1) @ w_out if need_ctx_out else None
    return out_lat, out_ctx


def _swiglu(h, w1, w3, w2):
    return (jax.nn.silu(h @ w1) * (h @ w3)) @ w2


def _moe_swiglu(h, router, w1, w3, w2):
    shp = h.shape
    x = h.reshape(-1, shp[-1])
    n, d = x.shape
    logits = (x @ router).astype(jnp.float32)
    top_logit, top_idx = lax.top_k(logits, TOP_K)
    gate = jax.nn.softmax(top_logit, axis=-1)
    a = n * TOP_K
    flat_e = top_idx.reshape(a)
    flat_tok = jnp.broadcast_to(jnp.arange(n, dtype=jnp.int32)[:, None], (n, TOP_K)).reshape(a)
    flat_g = gate.reshape(a)
    order = jnp.argsort(flat_e)
    e_sorted = flat_e[order]
    counts = jnp.bincount(flat_e, length=N_EXPERTS)
    starts = jnp.cumsum(counts) - counts
    padded = (counts + MOE_BLOCK - 1) // MOE_BLOCK * MOE_BLOCK
    pad_ends = jnp.cumsum(padded)
    pad_starts = pad_ends - padded
    dest = pad_starts[e_sorted] + (jnp.arange(a) - starts[e_sorted])
    n_blocks = -(-a // MOE_BLOCK) + N_EXPERTS
    n_slots = n_blocks * MOE_BLOCK
    slot_tok = jnp.full((n_slots,), n, jnp.int32).at[dest].set(flat_tok[order])
    slot_g = jnp.zeros((n_slots,), jnp.float32).at[dest].set(flat_g[order])
    block_e = jnp.minimum(jnp.searchsorted(pad_ends, jnp.arange(n_blocks) * MOE_BLOCK, side='right'), N_EXPERTS - 1)
    x_pad = jnp.concatenate([x, jnp.zeros((1, d), x.dtype)], axis=0)
    xs = x_pad[slot_tok].reshape(n_blocks, MOE_BLOCK, d)

    def run_block(args):
        xb, e = args
        return _swiglu(xb, w1[e], w3[e], w2[e])

    yb = lax.map(run_block, (xs, block_e)).reshape(n_slots, d)
    y = jnp.zeros((n + 1, d), x.dtype).at[slot_tok].add(yb * slot_g[:, None].astype(x.dtype))
    return y[:n].reshape(shp)


def setup_inputs(seed: int = 0) -> dict:
    key = jax.random.key(seed)
    ks = jax.random.split(key, 32)
    f32 = jnp.float32
    n_dense = (DEPTH + 1) // 2
    n_moe = DEPTH // 2

    def nrm(k, shape, s):
        return jax.random.normal(k, shape, f32) * s

    return {
        'x': nrm(ks[0], (BATCH, SEQ, D_MODEL), 1.0),
        'c': nrm(ks[1], (BATCH, D_MODEL), 1.0),
        'ctx': nrm(ks[2], (BATCH, CTX_LEN, D_MODEL), 1.0),
        'c_ctx': nrm(ks[3], (D_MODEL,), 1.0),
        'ada_w': nrm(ks[4], (DEPTH, D_MODEL, 6 * D_MODEL), 0.5 * D_MODEL ** -0.5),
        'ada_b': nrm(ks[5], (DEPTH, 6 * D_MODEL), 0.01),
        'norm_mix_g': 1.0 + nrm(ks[6], (DEPTH, D_MODEL), 0.05),
        'norm_ffn_g': 1.0 + nrm(ks[7], (DEPTH, D_MODEL), 0.05),
        'w_in': nrm(ks[8], (DEPTH, D_MODEL, D_IN), D_MODEL ** -0.5),
        'shift_mu': jax.random.uniform(ks[9], (DEPTH, D_SHIFT), f32),
        'w0': jax.random.uniform(ks[10], (DEPTH, 2, D_RWKV), f32, -6.0, -1.0),
        'w2': nrm(ks[11], (DEPTH, 2, R_DECAY, D_RWKV), 0.5 * R_DECAY ** -0.5),
        'a0': nrm(ks[12], (DEPTH, 2, D_RWKV), 0.1),
        'a2': nrm(ks[13], (DEPTH, 2, R_AAA, D_RWKV), 0.5 * R_AAA ** -0.5),
        'g2': nrm(ks[14], (DEPTH, R_GATE, D_RWKV), R_GATE ** -0.5),
        'k_k': 0.85 + nrm(ks[15], (DEPTH, D_RWKV), 0.05),
        'k_a': 1.0 + nrm(ks[16], (DEPTH, D_RWKV), 0.05),
        'r_k': nrm(ks[17], (DEPTH, H_RWKV, HEAD_DIM), 0.1),
        'ln_x_w': 1.0 + nrm(ks[18], (DEPTH, D_RWKV), 0.05),
        'ln_x_b': nrm(ks[19], (DEPTH, D_RWKV), 0.01),
        'na_rpb': nrm(ks[20], (DEPTH, H_NA, 2 * NA_ROWS_MAX - 1, 2 * NA_COLS - 1), 0.1),
        'w_out': nrm(ks[21], (DEPTH, D_MODEL, D_MODEL), D_MODEL ** -0.5),
        'ffn_w1': nrm(ks[22], (n_dense, D_MODEL, D_FF), D_MODEL ** -0.5),
        'ffn_w3': nrm(ks[23], (n_dense, D_MODEL, D_FF), D_MODEL ** -0.5),
        'ffn_w2': nrm(ks[24], (n_dense, D_FF, D_MODEL), D_FF ** -0.5),
        'router': nrm(ks[25], (n_moe, D_MODEL, N_EXPERTS), D_MODEL ** -0.5),
        'moe_w1': nrm(ks[26], (n_moe, N_EXPERTS, D_MODEL, D_FF_EXPERT), D_MODEL ** -0.5),
        'moe_w3': nrm(ks[27], (n_moe, N_EXPERTS, D_MODEL, D_FF_EXPERT), D_MODEL ** -0.5),
        'moe_w2': nrm(ks[28], (n_moe, N_EXPERTS, D_FF_EXPERT, D_MODEL), D_FF_EXPERT ** -0.5),
        'final_g': 1.0 + nrm(ks[29], (D_MODEL,), 0.05),
    }


def reference(x, c, ctx, c_ctx, ada_w, ada_b, norm_mix_g, norm_ffn_g, w_in, shift_mu, w0, w2, a0, a2, g2,
              k_k, k_a, r_k, ln_x_w, ln_x_b, na_rpb, w_out, ffn_w1, ffn_w3, ffn_w2, router, moe_w1, moe_w3,
              moe_w2, final_g):
    L = ctx.shape[1]
    s_c = jax.nn.silu(c)
    s_cc = jax.nn.silu(c_ctx)
    for l in range(DEPTH):
        last = l == DEPTH - 1
        mod = jnp.split((s_c @ ada_w[l] + ada_b[l])[:, None, :], 6, axis=-1)
        mod_c = jnp.split(s_cc @ ada_w[l] + ada_b[l], 6, axis=-1)
        h = _rms_norm(x, norm_mix_g[l]) * (1.0 + mod[1]) + mod[0]
        hc = _rms_norm(ctx, norm_mix_g[l]) * (1.0 + mod_c[1]) + mod_c[0]
        o, oc = _mixer(h, hc, w_in[l], shift_mu[l], w0[l], w2[l], a0[l], a2[l], g2[l], k_k[l], k_a[l], r_k[l],
                       ln_x_w[l], ln_x_b[l], na_rpb[l], w_out[l], not last)
        x = x + mod[2] * o
        h = _rms_norm(x, norm_ffn_g[l]) * (1.0 + mod[4]) + mod[3]
        if not last:
            ctx = ctx + mod_c[2] * oc
            hc = _rms_norm(ctx, norm_ffn_g[l]) * (1.0 + mod_c[4]) + mod_c[3]
            tokens = jnp.concatenate([hc, h], axis=1)
        else:
            tokens = h
        i = l // 2
        if l % 2 == 0:
            y = _swiglu(tokens, ffn_w1[i], ffn_w3[i], ffn_w2[i])
        else:
            y = _moe_swiglu(tokens, router[i], moe_w1[i], moe_w3[i], moe_w2[i])
        if not last:
            ctx = ctx + mod_c[5] * y[:, :L]
            x = x + mod[5] * y[:, L:]
        else:
            x = x + mod[5] * y
    return _rms_norm(x, final_g)
```

```python
import functools

import numpy as np
import jax
import jax.numpy as jnp
from jax import lax
from jax.experimental import pallas as pl
from jax.experimental.pallas import tpu as pltpu

F32 = jnp.float32
BF16 = jnp.bfloat16
HI = lax.Precision.HIGHEST

D_MODEL = 1024
HEAD_DIM = 64
GRID_W = 64
D_RWKV = 512
D_NA = 512
H_RWKV = D_RWKV // HEAD_DIM
H_NA = D_NA // HEAD_DIM
R_DECAY = 64
R_AAA = 64
R_GATE = 128
D_SHIFT = 3 * D_RWKV + 2 * R_DECAY + 2 * R_AAA + R_GATE
N_QKV = 3 * D_NA
N_EXPERTS = 8
MOE_BLOCK = 256
RMS_EPS = 1e-6
GN_EPS = 64e-5
L2_EPS = 1e-12
NEG = -1e30

TM = 256
NA_WIN_ROWS = 12
N_FEAT = 9
TC_SCAN = 16
LANES = 128


def _cparams(sem, vmem_mb=48):
    return pltpu.CompilerParams(dimension_semantics=sem, vmem_limit_bytes=vmem_mb * 1024 * 1024)


def _sigmoid(x):
    return 1.0 / (1.0 + jnp.exp(-x))


def _silu(x):
    return x * _sigmoid(x)


def _softplus(x):
    return jnp.maximum(x, 0.0) + jnp.log(1.0 + jnp.exp(-jnp.abs(x)))


def _rms(x):
    return x * lax.rsqrt(jnp.mean(x * x, axis=-1, keepdims=True) + RMS_EPS)


def _norm_mod(x, g, scale, shift):
    return _rms(x) * g * (1.0 + scale) + shift


def _head_sum(x, bd):
    hi = x.astype(BF16)
    lo = (x - hi.astype(F32)).astype(BF16)
    return jnp.dot(hi, bd, preferred_element_type=F32) + jnp.dot(lo, bd, preferred_element_type=F32)


def _dot_nt(a, b):
    return lax.dot_general(a, b, (((1,), (1,)), ((), ())), preferred_element_type=F32)


def _mods_kernel(c_ref, w_ref, b_ref, o_ref):
    o_ref[0] = jnp.dot(_silu(c_ref[...]), w_ref[0], precision=HI, preferred_element_type=F32) + b_ref[0]


def _mods(c_all, ada_w, ada_b):
    depth, d, n6 = ada_w.shape
    nb = c_all.shape[0]
    tn = 1536
    return pl.pallas_call(
        _mods_kernel,
        grid=(depth, n6 // tn),
        in_specs=[pl.BlockSpec((nb, d), lambda l, n: (0, 0)),
                  pl.BlockSpec((1, d, tn), lambda l, n: (l, 0, n)),
                  pl.BlockSpec((1, 1, tn), lambda l, n: (l, 0, n))],
        out_specs=pl.BlockSpec((1, nb, tn), lambda l, n: (l, 0, n)),
        out_shape=jax.ShapeDtypeStruct((depth, nb, n6), F32),
        compiler_params=_cparams(("arbitrary", "arbitrary")),
        name="adaln_mods",
    )(c_all, ada_w, ada_b.reshape(depth, 1, n6))


def _mod_spec(n_ctx_tiles):
    return pl.BlockSpec((None, None, 6, D_MODEL), lambda b, j: (b, jnp.where(j >= n_ctx_tiles, 1, 0), 0, 0))


def _inproj_kernel(x_ref, mod_ref, g_ref, w_ref, qkv_ref, u_ref):
    h = _norm_mod(x_ref[0], g_ref[...], mod_ref[1:2, :], mod_ref[0:1, :]).astype(BF16)
    qkv_ref[0] = jnp.dot(h, w_ref[:, :N_QKV], preferred_element_type=F32).astype(BF16)
    u_ref[0] = jnp.dot(h, w_ref[:, N_QKV:], preferred_element_type=F32)


def _inproj(xs, modtab, g, w_in_bf, n_ctx_tiles):
    b, s, d = xs.shape
    nj = s // TM
    return pl.pallas_call(
        _inproj_kernel,
        grid=(b, nj),
        in_specs=[pl.BlockSpec((1, TM, d), lambda b, j: (b, j, 0)),
                  _mod_spec(n_ctx_tiles),
                  pl.BlockSpec((1, d), lambda b, j: (0, 0)),
                  pl.BlockSpec((d, N_QKV + D_SHIFT), lambda b, j: (0, 0))],
        out_specs=[pl.BlockSpec((1, TM, N_QKV), lambda b, j: (b, j, 0)),
                   pl.BlockSpec((1, TM, D_SHIFT), lambda b, j: (b, j, 0))],
        out_shape=[jax.ShapeDtypeStruct((b, s, N_QKV), BF16),
                   jax.ShapeDtypeStruct((b, s, D_SHIFT), F32)],
        compiler_params=_cparams(("arbitrary", "arbitrary")),
        name="in_proj",
    )(xs, modtab, g.reshape(1, d), w_in_bf)


def _na_bias_table(rpb):
    i = np.arange(4)[:, None, None, None]
    cq = np.arange(GRID_W)[None, :, None, None]
    wr = np.arange(NA_WIN_ROWS)[None, None, :, None]
    ck = np.arange(GRID_W)[None, None, None, :]
    c0 = np.clip(cq - 8, 0, GRID_W - 16)
    col_ok = (ck >= c0) & (ck < c0 + 16)
    col_off = np.clip(ck - cq + 15, 0, 30)
    shape = (4, GRID_W, NA_WIN_ROWS, GRID_W)
    tabs = []
    for case in range(3):
        if case == 0:
            ii, row_off = wr + 0 * i, wr - i + 7
        elif case == 1:
            ii, row_off = wr - i, wr - i + 3
        else:
            ii, row_off = wr - 4 + 0 * i, wr - i - 1
        ok = np.broadcast_to((ii >= 0) & (ii < 8) & col_ok, shape)
        ro = np.broadcast_to(np.clip(row_off, 0, 14), shape)
        co = np.broadcast_to(col_off, shape)
        vals = rpb[:, ro, co]
        tabs.append(jnp.where(ok[None], vals, NEG).reshape(rpb.shape[0], TM, NA_WIN_ROWS * GRID_W))
    return jnp.stack(tabs, axis=1)


def _na_kernel(q_ref, k_ref, v_ref, bias_ref, o_ref, *, n_ctx, rows):
    j = pl.program_id(1)
    n_ctx_tiles = n_ctx // TM
    scale = HEAD_DIM ** -0.5
    nwin = NA_WIN_ROWS * GRID_W

    def head(h):
        return slice(h * HEAD_DIM, (h + 1) * HEAD_DIM)

    @pl.when(j < n_ctx_tiles)
    def _ctx():
        for h in range(H_NA):
            q = q_ref[0, :, head(h)]
            s = _dot_nt(q, k_ref[0, 0:n_ctx, head(h)]) * scale
            p = jnp.exp(s - jnp.max(s, axis=-1, keepdims=True))
            y = jnp.dot(p.astype(BF16), v_ref[0, 0:n_ctx, head(h)], preferred_element_type=F32)
            o_ref[0, :, head(h)] = (y / jnp.sum(p, axis=-1, keepdims=True)).astype(BF16)

    @pl.when(j >= n_ctx_tiles)
    def _lat():
        m = j - n_ctx_tiles
        w0 = jnp.clip(4 * m - 4, 0, rows - NA_WIN_ROWS)
        start = pl.multiple_of(n_ctx + w0 * GRID_W, GRID_W)
        for h in range(H_NA):
            q = q_ref[0, :, head(h)]
            s_loc = _dot_nt(q, k_ref[0, pl.ds(start, nwin), head(h)]) * scale + bias_ref[h]
            s_ctx = _dot_nt(q, k_ref[0, 0:n_ctx, head(h)]) * scale
            mx = jnp.maximum(jnp.max(s_loc, axis=-1, keepdims=True), jnp.max(s_ctx, axis=-1, keepdims=True))
            p_loc = jnp.exp(s_loc - mx)
            p_ctx = jnp.exp(s_ctx - mx)
            den = jnp.sum(p_loc, axis=-1, keepdims=True) + jnp.sum(p_ctx, axis=-1, keepdims=True)
            y = (jnp.dot(p_loc.astype(BF16), v_ref[0, pl.ds(start, nwin), head(h)], preferred_element_type=F32)
                 + jnp.dot(p_ctx.astype(BF16), v_ref[0, 0:n_ctx, head(h)], preferred_element_type=F32))
            o_ref[0, :, head(h)] = (y / den).astype(BF16)


def _na(qkv, bias_tab, n_ctx):
    b, s, _ = qkv.shape
    nj = s // TM
    n_ctx_tiles = n_ctx // TM
    rows = (s - n_ctx) // GRID_W
    n_lat_tiles = nj - n_ctx_tiles
    assert rows >= NA_WIN_ROWS and rows % 4 == 0 and n_lat_tiles >= 3

    def case_map(b, j):
        m = j - n_ctx_tiles
        return (0, jnp.where(m <= 0, 0, jnp.where(m >= n_lat_tiles - 1, 2, 1)), 0, 0)

    return pl.pallas_call(
        functools.partial(_na_kernel, n_ctx=n_ctx, rows=rows),
        grid=(b, nj),
        in_specs=[pl.BlockSpec((1, TM, D_NA), lambda b, j: (b, j, 0)),
                  pl.BlockSpec((1, s, D_NA), lambda b, j: (b, 0, 1)),
                  pl.BlockSpec((1, s, D_NA), lambda b, j: (b, 0, 2)),
                  pl.BlockSpec((H_NA, None, TM, NA_WIN_ROWS * GRID_W), case_map)],
        out_specs=pl.BlockSpec((1, TM, D_NA), lambda b, j: (b, j, 0)),
        out_shape=jax.ShapeDtypeStruct((b, s, D_NA), BF16),
        compiler_params=_cparams(("arbitrary", "arbitrary")),
        name="na_attention",
    )(qkv, qkv, qkv, bias_tab)


def _feat_kernel(u_ref, up_ref, un_ref, mu_ref, w0_ref, w2_ref, a0_ref, a2_ref, g2_ref, kk_ref, ka_ref, rk_ref,
                 bd_ref, feat_ref, rd_ref, *, n_ctx_tiles, nj):
    j = pl.program_id(1)
    seg_start = (j == 0) | (j == n_ctx_tiles)
    seg_end = (j == n_ctx_tiles - 1) | (j == nj - 1)
    rid = lax.broadcasted_iota(jnp.int32, (TM, 1), 0)

    def shifted(lo, hi):
        p = u_ref[0, :, lo:hi]
        prev_row = jnp.where(seg_start, 0.0, up_ref[0, 7:8, lo:hi])
        next_row = jnp.where(seg_end, 0.0, un_ref[0, 0:1, lo:hi])
        prev = jnp.where(rid == 0, prev_row, pltpu.roll(p, 1, axis=0))
        nxt = jnp.where(rid == TM - 1, next_row, pltpu.roll(p, TM - 1, axis=0))
        return p + mu_ref[:, lo:hi] * (0.5 * (prev + nxt) - p)

    c = D_RWKV
    r = shifted(0, c)
    k = shifted(c, 2 * c)
    v = shifted(2 * c, 3 * c)
    o = 3 * c
    wl = shifted(o, o + 2 * R_DECAY)
    al = shifted(o + 2 * R_DECAY, o + 2 * R_DECAY + 2 * R_AAA)
    gl = shifted(o + 2 * R_DECAY + 2 * R_AAA, D_SHIFT)
    bd = bd_ref[...]

    lw = jnp.dot(jnp.tanh(wl), w2_ref[...], precision=HI, preferred_element_type=F32)
    la = jnp.dot(al, a2_ref[...], precision=HI, preferred_element_type=F32)
    g = jnp.dot(_sigmoid(gl), g2_ref[...], precision=HI, preferred_element_type=F32)

    kks = k * kk_ref[...]
    kk = kks / jnp.maximum(jnp.sqrt(_head_sum(kks * kks, bd)), L2_EPS)
    feat_ref[0, :, 0:c] = r
    feat_ref[0, :, c:2 * c] = v
    feat_ref[0, :, 2 * c:3 * c] = kk
    ksum = jnp.zeros_like(k)
    for z in range(2):
        w_log = -_softplus(-(w0_ref[z:z + 1, :] + lw[:, z * c:(z + 1) * c])) - 0.5
        decay = jnp.exp(-jnp.exp(w_log))
        a = _sigmoid(a0_ref[z:z + 1, :] + la[:, z * c:(z + 1) * c])
        k_dir = k * (1.0 + (a - 1.0) * ka_ref[...])
        ksum = ksum + k_dir
        base = (3 + 3 * z) * c
        feat_ref[0, :, base:base + c] = decay
        feat_ref[0, :, base + c:base + 2 * c] = k_dir
        feat_ref[0, :, base + 2 * c:base + 3 * c] = kk * a
    bonus = _head_sum(r * (0.5 * ksum) * rk_ref[...], bd) * v
    rd_ref[0, :, 0:c] = bonus
    rd_ref[0, :, c:2 * c] = g


def _features(u_raw, n_ctx, shift_mu, w0, w2blk, a0, a2blk, g2, k_k, k_a, r_k, bd):
    b, s, _ = u_raw.shape
    nj = s // TM
    n_ctx_tiles = n_ctx // TM
    c = D_RWKV
    t8 = TM // 8

    def full(shape):
        return pl.BlockSpec(shape, lambda b, j: (0,) * len(shape))

    return pl.pallas_call(
        functools.partial(_feat_kernel, n_ctx_tiles=n_ctx_tiles, nj=nj),
        grid=(b, nj),
        in_specs=[pl.BlockSpec((1, TM, D_SHIFT), lambda b, j: (b, j, 0)),
                  pl.BlockSpec((1, 8, D_SHIFT), lambda b, j: (b, jnp.maximum(j * t8 - 1, 0), 0)),
                  pl.BlockSpec((1, 8, D_SHIFT), lambda b, j: (b, jnp.minimum((j + 1) * t8, s // 8 - 1), 0)),
                  full((1, D_SHIFT)), full((2, c)), full((2 * R_DECAY, 2 * c)), full((2, c)),
                  full((2 * R_AAA, 2 * c)), full((R_GATE, c)), full((1, c)), full((1, c)), full((1, c)),
                  full((c, c))],
        out_specs=[pl.BlockSpec((1, TM, N_FEAT * c), lambda b, j: (b, j, 0)),
                   pl.BlockSpec((1, TM, 2 * c), lambda b, j: (b, j, 0))],
        out_shape=[jax.ShapeDtypeStruct((b, s, N_FEAT * c), F32),
                   jax.ShapeDtypeStruct((b, s, 2 * c), F32)],
        compiler_params=_cparams(("arbitrary", "arbitrary")),
        name="rwkv_features",
    )(u_raw, u_raw, u_raw, shift_mu.reshape(1, D_SHIFT), w0, w2blk, a0, a2blk, g2, k_k.reshape(1, c),
      k_a.reshape(1, c), r_k.reshape(1, c), bd)


def _scan_kernel(fs_ref, fd_ref, bs_ref, bd_ref, yf_ref, yb_ref, st_ref):
    @pl.when(pl.program_id(0) == 0)
    def _init():
        st_ref[...] = jnp.zeros_like(st_ref)

    def one_step(d, s_ref, d_ref, y_ref, row):
        v_t = s_ref[row, 1]
        sa = jnp.zeros_like(v_t)
        for k in range(HEAD_DIM):
            sa = sa + st_ref[d, k] * s_ref[row, 2, pl.ds(k, 1), :]
        y = jnp.zeros_like(v_t)
        for k in range(HEAD_DIM):
            s_new = (st_ref[d, k] * d_ref[row, 0, pl.ds(k, 1), :] - sa * d_ref[row, 2, pl.ds(k, 1), :]
                     + v_t * d_ref[row, 1, pl.ds(k, 1), :])
            st_ref[d, k] = s_new
            y = y + s_new * s_ref[row, 0, pl.ds(k, 1), :]
        y_ref[row] = y

    def step(jj, carry):
        one_step(0, fs_ref, fd_ref, yf_ref, jj)
        one_step(1, bs_ref, bd_ref, yb_ref, TC_SCAN - 1 - jj)
        return carry

    lax.fori_loop(0, TC_SCAN, step, 0)


def _scan(feat_t, n_ctx):
    s, _, hd, lanes = feat_t.shape
    nblk = s // TC_SCAN
    nc = n_ctx // TC_SCAN
    assert n_ctx % TC_SCAN == 0 and s % TC_SCAN == 0

    def bwd(i):
        return jnp.where(i < nc, nc - 1 - i, nblk - 1 - i + nc)

    blk = (TC_SCAN, 3, hd, lanes)
    yblk = (TC_SCAN, hd, lanes)
    return pl.pallas_call(
        _scan_kernel,
        grid=(nblk,),
        in_specs=[pl.BlockSpec(blk, lambda i: (i, 0, 0, 0)),
                  pl.BlockSpec(blk, lambda i: (i, 1, 0, 0)),
                  pl.BlockSpec(blk, lambda i: (bwd(i), 0, 0, 0)),
                  pl.BlockSpec(blk, lambda i: (bwd(i), 2, 0, 0))],
        out_specs=[pl.BlockSpec(yblk, lambda i: (i, 0, 0)),
                   pl.BlockSpec(yblk, lambda i: (bwd(i), 0, 0))],
        out_shape=[jax.ShapeDtypeStruct((s, hd, lanes), F32)] * 2,
        scratch_shapes=[pltpu.VMEM((2, HEAD_DIM, hd, lanes), F32)],
        compiler_params=_cparams(("arbitrary",)),
        name="rwkv_scan",
    )(feat_t, feat_t, feat_t, feat_t)


def _outproj_kernel(yf_ref, yb_ref, rd_ref, yna_ref, x_ref, mod_ref, lnw_ref, lnb_ref, bd_ref, w_ref, o_ref):
    c = D_RWKV
    bd = bd_ref[...]
    y = yf_ref[0] + yb_ref[0]
    mu = _head_sum(y, bd) * (1.0 / HEAD_DIM)
    dlt = y - mu
    var = _head_sum(dlt * dlt, bd) * (1.0 / HEAD_DIM)
    yn = dlt * lax.rsqrt(var + GN_EPS) * lnw_ref[...] + lnb_ref[...]
    y_rw = ((yn + rd_ref[0, :, 0:c]) * rd_ref[0, :, c:2 * c]).astype(BF16)
    o = (jnp.dot(y_rw, w_ref[0:c, :], preferred_element_type=F32)
         + jnp.dot(yna_ref[0], w_ref[c:, :], preferred_element_type=F32))
    o_ref[0] = x_ref[0] + mod_ref[2:3, :] * o


def _outproj(yf, yb, rd, yna, xs, modtab, ln_w, ln_b, bd, w_out_bf, n_ctx_tiles):
    b, s, d = xs.shape
    nj = s // TM
    c = D_RWKV

    def tile(w):
        return pl.BlockSpec((1, TM, w), lambda b, j: (b, j, 0))

    def full(shape):
        return pl.BlockSpec(shape, lambda b, j: (0,) * len(shape))

    return pl.pallas_call(
        _outproj_kernel,
        grid=(b, nj),
        in_specs=[tile(c), tile(c), tile(2 * c), tile(D_NA), tile(d), _mod_spec(n_ctx_tiles),
                  full((1, c)), full((1, c)), full((c, c)), full((d, d))],
        out_specs=tile(d),
        out_shape=jax.ShapeDtypeStruct((b, s, d), F32),
        compiler_params=_cparams(("arbitrary", "arbitrary")),
        name="out_proj",
    )(yf, yb, rd, yna, xs, modtab, ln_w.reshape(1, c), ln_b.reshape(1, c), bd, w_out_bf)


def _ffn_kernel(x_ref, mod_ref, g_ref, w1_ref, w3_ref, w2_ref, o_ref, *, ffc):
    x = x_ref[0]
    h = _norm_mod(x, g_ref[...], mod_ref[4:5, :], mod_ref[3:4, :]).astype(BF16)
    d_ff = w1_ref.shape[1]
    acc = jnp.zeros(x.shape, F32)
    for c0 in range(0, d_ff, ffc):
        a = jnp.dot(h, w1_ref[:, c0:c0 + ffc], preferred_element_type=F32)
        g = jnp.dot(h, w3_ref[:, c0:c0 + ffc], preferred_element_type=F32)
        acc = acc + jnp.dot((_silu(a) * g).astype(BF16), w2_ref[c0:c0 + ffc, :], preferred_element_type=F32)
    o_ref[0] = x + mod_ref[5:6, :] * acc


def _ffn(xs, modtab, g, w1, w3, w2, n_ctx_tiles):
    b, s, d = xs.shape
    nj = s // TM
    d_ff = w1.shape[1]
    const = lambda b, j: (0, 0)
    return pl.pallas_call(
        functools.partial(_ffn_kernel, ffc=d_ff // 2),
        grid=(b, nj),
        in_specs=[pl.BlockSpec((1, TM, d), lambda b, j: (b, j, 0)), _mod_spec(n_ctx_tiles),
                  pl.BlockSpec((1, d), const),
                  pl.BlockSpec((d, d_ff), const), pl.BlockSpec((d, d_ff), const), pl.BlockSpec((d_ff, d), const)],
        out_specs=pl.BlockSpec((1, TM, d), lambda b, j: (b, j, 0)),
        out_shape=jax.ShapeDtypeStruct((b, s, d), F32),
        compiler_params=_cparams(("arbitrary", "arbitrary"), 56),
        name="ffn_swiglu",
    )(xs, modtab, g.reshape(1, d), w1, w3, w2)


def _router_kernel(x_ref, mod_ref, g_ref, r_ref, h_ref, slab_ref):
    h = _norm_mod(x_ref[0], g_ref[...], mod_ref[4:5, :], mod_ref[3:4, :])
    h_ref[0] = h
    logits = jnp.dot(h, r_ref[...], precision=HI, preferred_element_type=F32)
    lane = lax.broadcasted_iota(jnp.int32, logits.shape, 1)
    lg = jnp.where(lane < N_EXPERTS, logits, -jnp.inf)
    m1 = jnp.max(lg, axis=-1, keepdims=True)
    i1 = jnp.min(jnp.where(lg == m1, lane, LANES), axis=-1, keepdims=True)
    lg2 = jnp.where(lane == i1, -jnp.inf, lg)
    m2 = jnp.max(lg2, axis=-1, keepdims=True)
    i2 = jnp.min(jnp.where(lg2 == m2, lane, LANES), axis=-1, keepdims=True)
    e = jnp.exp(m2 - m1)
    g1 = 1.0 / (1.0 + e)
    g2 = e / (1.0 + e)
    slab_ref[0] = jnp.where(lane == 0, i1.astype(F32),
                            jnp.where(lane == 1, i2.astype(F32),
                                      jnp.where(lane == 2, g1, jnp.where(lane == 3, g2, 0.0))))


def _router(xs, modtab, g, router_pad, n_ctx_tiles):
    b, s, d = xs.shape
    nj = s // TM
    const = lambda b, j: (0, 0)
    return pl.pallas_call(
        _router_kernel,
        grid=(b, nj),
        in_specs=[pl.BlockSpec((1, TM, d), lambda b, j: (b, j, 0)), _mod_spec(n_ctx_tiles),
                  pl.BlockSpec((1, d), const), pl.BlockSpec((d, LANES), const)],
        out_specs=[pl.BlockSpec((1, TM, d), lambda b, j: (b, j, 0)),
                   pl.BlockSpec((1, TM, LANES), lambda b, j: (b, j, 0))],
        out_shape=[jax.ShapeDtypeStruct((b, s, d), F32), jax.ShapeDtypeStruct((b, s, LANES), F32)],
        compiler_params=_cparams(("arbitrary", "arbitrary")),
        name="moe_router",
    )(xs, modtab, g.reshape(1, d), router_pad)


def _dispatch(idx):
    n = idx.shape[0]
    a = n * 2
    flat_e = idx.reshape(a)
    onehot = (flat_e[None, :] == jnp.arange(N_EXPERTS, dtype=jnp.int32)[:, None]).astype(jnp.int32)
    csum = jnp.cumsum(onehot, axis=1)
    counts = csum[:, -1]
    rank = jnp.sum(csum * onehot, axis=0) - 1
    padded = (counts + MOE_BLOCK - 1) // MOE_BLOCK * MOE_BLOCK
    pad_ends = jnp.cumsum(padded)
    pad_starts = pad_ends - padded
    dest = (pad_starts[flat_e] + rank).astype(jnp.int32)
    n_blocks = -(-a // MOE_BLOCK) + N_EXPERTS
    slot_tok = jnp.zeros((n_blocks * MOE_BLOCK,), jnp.int32).at[dest].set(jnp.arange(a, dtype=jnp.int32) // 2)
    block_e = jnp.minimum(jnp.searchsorted(pad_ends, jnp.arange(n_blocks) * MOE_BLOCK, side='right'),
                          N_EXPERTS - 1).astype(jnp.int32)
    n_used = (pad_ends[-1] // MOE_BLOCK).astype(jnp.int32).reshape(1)
    return slot_tok, block_e, n_used, dest


def _moe_mm_kernel(be_ref, st_ref, nu_ref, h_hbm, w1_hbm, w3_hbm, w2_hbm, o_ref, xbuf, w1s, w3s, w2s, gsem, wsem,
                   *, ffc):
    i = pl.program_id(0)
    n_used = nu_ref[0]

    def row_copy(tok, slot, r):
        return pltpu.make_async_copy(h_hbm.at[pl.ds(tok, 1)], xbuf.at[slot, pl.ds(r, 1)], gsem.at[slot])

    def issue(blk, slot):
        def body(r, carry):
            row_copy(st_ref[blk * MOE_BLOCK + r], slot, r).start()
            return carry
        lax.fori_loop(0, MOE_BLOCK, body, 0)

    def wait_rows(slot):
        def body(r, carry):
            row_copy(0, slot, r).wait()
            return carry
        lax.fori_loop(0, MOE_BLOCK, body, 0)

    @pl.when(i == 0)
    def _first():
        issue(0, 0)

    @pl.when(i + 1 < n_used)
    def _prefetch():
        issue(i + 1, lax.rem(i + 1, 2))

    @pl.when(i < n_used)
    def _compute():
        e = be_ref[i]
        changed = (i == 0) | (e != be_ref[jnp.maximum(i - 1, 0)])

        @pl.when(changed)
        def _load_weights():
            copies = [pltpu.make_async_copy(w1_hbm.at[e], w1s, wsem.at[0]),
                      pltpu.make_async_copy(w3_hbm.at[e], w3s, wsem.at[1]),
                      pltpu.make_async_copy(w2_hbm.at[e], w2s, wsem.at[2])]
            for cp in copies:
                cp.start()
            for cp in copies:
                cp.wait()

        slot = lax.rem(i, 2)
        wait_rows(slot)
        xb = xbuf[slot].astype(BF16)
        acc = jnp.zeros(o_ref.shape, F32)
        for c0 in range(0, w1s.shape[1], ffc):
            a = jnp.dot(xb, w1s[:, c0:c0 + ffc], preferred_element_type=F32)
            g = jnp.dot(xb, w3s[:, c0:c0 + ffc], preferred_element_type=F32)
            acc = acc + jnp.dot((_silu(a) * g).astype(BF16), w2s[c0:c0 + ffc, :], preferred_element_type=F32)
        o_ref[...] = acc

    @pl.when(i >= n_used)
    def _unused():
        o_ref[...] = jnp.zeros_like(o_ref)


def _moe_mm(h_flat, slot_tok, block_e, n_used, w1, w3, w2):
    n, d = h_flat.shape
    n_blocks = block_e.shape[0]
    d_ff = w1.shape[2]
    any_spec = pl.BlockSpec(memory_space=pl.ANY)
    return pl.pallas_call(
        functools.partial(_moe_mm_kernel, ffc=512),
        grid_spec=pltpu.PrefetchScalarGridSpec(
            num_scalar_prefetch=3,
            grid=(n_blocks,),
            in_specs=[any_spec, any_spec, any_spec, any_spec],
            out_specs=pl.BlockSpec((MOE_BLOCK, d), lambda i, be, st, nu: (i, 0)),
            scratch_shapes=[pltpu.VMEM((2, MOE_BLOCK, d), F32),
                            pltpu.VMEM((d, d_ff), BF16), pltpu.VMEM((d, d_ff), BF16), pltpu.VMEM((d_ff, d), BF16),
                            pltpu.SemaphoreType.DMA((2,)), pltpu.SemaphoreType.DMA((3,))]),
        out_shape=jax.ShapeDtypeStruct((n_blocks * MOE_BLOCK, d), F32),
        compiler_params=_cparams(("arbitrary",), 56),
        name="moe_grouped_swiglu",
    )(block_e, slot_tok, n_used, h_flat, w1, w3, w2)


def _moe_comb_kernel(inv_ref, x_ref, mod_ref, slab_ref, fg_ref, yb_hbm, o_ref, gbuf, gsem, *, nj, final):
    b = pl.program_id(0)
    j = pl.program_id(1)
    t = b * nj + j
    nt = pl.num_programs(0) * nj

    def row_copy(src, slot, which, r):
        return pltpu.make_async_copy(yb_hbm.at[pl.ds(src, 1)], gbuf.at[slot, which, pl.ds(r, 1)], gsem.at[slot])

    def issue(tile, slot):
        def body(r, carry):
            base = (tile * TM + r) * 2
            row_copy(inv_ref[base], slot, 0, r).start()
            row_copy(inv_ref[base + 1], slot, 1, r).start()
            return carry
        lax.fori_loop(0, TM, body, 0)

    def wait_rows(slot):
        def body(r, carry):
            row_copy(0, slot, 0, r).wait()
            row_copy(0, slot, 1, r).wait()
            return carry
        lax.fori_loop(0, TM, body, 0)

    @pl.when(t == 0)
    def _first():
        issue(0, 0)

    @pl.when(t + 1 < nt)
    def _prefetch():
        issue(t + 1, lax.rem(t + 1, 2))

    slot = lax.rem(t, 2)
    wait_rows(slot)
    slab = slab_ref[0]
    y = slab[:, 2:3] * gbuf[slot, 0] + slab[:, 3:4] * gbuf[slot, 1]
    xn = x_ref[0] + mod_ref[5:6, :] * y
    if final:
        xn = _rms(xn) * fg_ref[...]
    o_ref[0] = xn


def _moe_combine(inv, xs, modtab, slab, final_g, yb, n_ctx_tiles, final):
    b, s, d = xs.shape
    nj = s // TM
    return pl.pallas_call(
        functools.partial(_moe_comb_kernel, nj=nj, final=final),
        grid_spec=pltpu.PrefetchScalarGridSpec(
            num_scalar_prefetch=1,
            grid=(b, nj),
            in_specs=[pl.BlockSpec((1, TM, d), lambda b, j, inv: (b, j, 0)),
                      pl.BlockSpec((None, None, 6, D_MODEL),
                                   lambda b, j, inv: (b, jnp.where(j >= n_ctx_tiles, 1, 0), 0, 0)),
                      pl.BlockSpec((1, TM, LANES), lambda b, j, inv: (b, j, 0)),
                      pl.BlockSpec((1, d), lambda b, j, inv: (0, 0)),
                      pl.BlockSpec(memory_space=pl.ANY)],
            out_specs=pl.BlockSpec((1, TM, d), lambda b, j, inv: (b, j, 0)),
            scratch_shapes=[pltpu.VMEM((2, 2, TM, d), F32), pltpu.SemaphoreType.DMA((2,))]),
        out_shape=jax.ShapeDtypeStruct((b, s, d), F32),
        compiler_params=_cparams(("arbitrary", "arbitrary")),
        name="moe_combine",
    )(inv, xs, modtab, slab, final_g.reshape(1, d), yb)


def _blockdiag(w):
    z = jnp.zeros_like(w[0])
    return jnp.concatenate([jnp.concatenate([w[0], z], axis=1), jnp.concatenate([z, w[1]], axis=1)], axis=0)


def kernel(x, c, ctx, c_ctx, ada_w, ada_b, norm_mix_g, norm_ffn_g, w_in, shift_mu, w0, w2, a0, a2, g2, k_k, k_a, r_k,
           ln_x_w, ln_x_b, na_rpb, w_out, ffn_w1, ffn_w3, ffn_w2, router, moe_w1, moe_w3, moe_w2, final_g):
    b, t, d = x.shape
    n_ctx = ctx.shape[1]
    s = n_ctx + t
    depth = ada_w.shape[0]
    assert d == D_MODEL and n_ctx % TM == 0 and t % TM == 0
    n_ctx_tiles = n_ctx // TM
    lanes = b * H_RWKV

    nb = -(-(b + 1) // 8) * 8
    c_all = jnp.concatenate([c, c_ctx[None, :], jnp.zeros((nb - b - 1, d), F32)], axis=0)
    mods = _mods(c_all, ada_w, ada_b).reshape(depth, nb, 6, d)
    modtab = jnp.stack([jnp.broadcast_to(mods[:, b:b + 1], (depth, b, 6, d)), mods[:, :b]], axis=2)

    head_id = np.arange(D_RWKV) // HEAD_DIM
    bd = jnp.asarray(head_id[:, None] == head_id[None, :], BF16)

    xs = jnp.concatenate([ctx, x], axis=1)
    for l in range(depth):
        last = l == depth - 1
        qkv, u_raw = _inproj(xs, modtab[l], norm_mix_g[l], w_in[l].astype(BF16), n_ctx_tiles)
        y_na = _na(qkv, _na_bias_table(na_rpb[l]), n_ctx)
        feat, rd = _features(u_raw, n_ctx, shift_mu[l], w0[l], _blockdiag(w2[l]), a0[l], _blockdiag(a2[l]), g2[l],
                             k_k[l], k_a[l], r_k[l], bd)
        feat_t = feat.reshape(b, s, N_FEAT, H_RWKV, HEAD_DIM).transpose(1, 2, 4, 0, 3).reshape(
            s, N_FEAT, HEAD_DIM, lanes)
        yf_t, yb_t = _scan(feat_t, n_ctx)

        def untranspose(y):
            return y.reshape(s, HEAD_DIM, b, H_RWKV).transpose(2, 0, 3, 1).reshape(b, s, D_RWKV)

        xs = _outproj(untranspose(yf_t), untranspose(yb_t), rd, y_na, xs, modtab[l], ln_x_w[l], ln_x_b[l], bd,
                      w_out[l].astype(BF16), n_ctx_tiles)
        i = l // 2
        if l % 2 == 0:
            xs = _ffn(xs, modtab[l], norm_ffn_g[l], ffn_w1[i].astype(BF16), ffn_w3[i].astype(BF16),
                      ffn_w2[i].astype(BF16), n_ctx_tiles)
        else:
            router_pad = jnp.pad(router[i], ((0, 0), (0, LANES - N_EXPERTS)))
            h, slab = _router(xs, modtab[l], norm_ffn_g[l], router_pad, n_ctx_tiles)
            idx = slab[..., 0:2].astype(jnp.int32).reshape(b * s, 2)
            slot_tok, block_e, n_used, dest = _dispatch(idx)
            yb = _moe_mm(h.reshape(b * s, d), slot_tok, block_e, n_used, moe_w1[i].astype(BF16),
                         moe_w3[i].astype(BF16), moe_w2[i].astype(BF16))
            xs = _moe_combine(dest, xs, modtab[l], slab, final_g, yb, n_ctx_tiles, final=last)
    if depth % 2 == 1:
        raise NotImplementedError("final norm is fused into the MoE combine of the last (odd) layer")
    return xs[:, n_ctx:, :]
```

```python
import functools

import numpy as np
import jax
import jax.numpy as jnp
from jax import lax
from jax.experimental import pallas as pl
from jax.experimental.pallas import tpu as pltpu

F32 = jnp.float32
BF16 = jnp.bfloat16
HI = lax.Precision.HIGHEST

D_MODEL = 1024
HEAD_DIM = 64
GRID_W = 64
D_RWKV = 512
D_NA = 512
H_RWKV = D_RWKV // HEAD_DIM
H_NA = D_NA // HEAD_DIM
R_DECAY = 64
R_AAA = 64
R_GATE = 128
D_SHIFT = 3 * D_RWKV + 2 * R_DECAY + 2 * R_AAA + R_GATE
N_QKV = 3 * D_NA
N_EXPERTS = 8
MOE_BLOCK = 256
RMS_EPS = 1e-6
GN_EPS = 64e-5
L2_EPS = 1e-12
NEG = -1e30

TM = 256
NA_WIN_ROWS = 12
N_FEAT = 9
TC_SCAN = 16
LANES = 128


def _cparams(sem, vmem_mb=48):
    return pltpu.CompilerParams(dimension_semantics=sem, vmem_limit_bytes=vmem_mb * 1024 * 1024)


def _sigmoid(x):
    return 1.0 / (1.0 + jnp.exp(-x))


def _silu(x):
    return x * _sigmoid(x)


def _softplus(x):
    return jnp.maximum(x, 0.0) + jnp.log(1.0 + jnp.exp(-jnp.abs(x)))


def _rms(x):
    return x * lax.rsqrt(jnp.mean(x * x, axis=-1, keepdims=True) + RMS_EPS)


def _norm_mod(x, g, scale, shift):
    return _rms(x) * g * (1.0 + scale) + shift


def _head_sum(x, bd):
    hi = x.astype(BF16)
    lo = (x - hi.astype(F32)).astype(BF16)
    return jnp.dot(hi, bd, preferred_element_type=F32) + jnp.dot(lo, bd, preferred_element_type=F32)


def _dot_nt(a, b):
    return lax.dot_general(a, b, (((1,), (1,)), ((), ())), preferred_element_type=F32)


def _mods_kernel(c_ref, w_ref, b_ref, o_ref):
    o_ref[0] = jnp.dot(_silu(c_ref[...]), w_ref[0], precision=HI, preferred_element_type=F32) + b_ref[0]


def _mods(c_all, ada_w, ada_b):
    depth, d, n6 = ada_w.shape
    nb = c_all.shape[0]
    tn = 1536
    return pl.pallas_call(
        _mods_kernel,
        grid=(depth, n6 // tn),
        in_specs=[pl.BlockSpec((nb, d), lambda l, n: (0, 0)),
                  pl.BlockSpec((1, d, tn), lambda l, n: (l, 0, n)),
                  pl.BlockSpec((1, 1, tn), lambda l, n: (l, 0, n))],
        out_specs=pl.BlockSpec((1, nb, tn), lambda l, n: (l, 0, n)),
        out_shape=jax.ShapeDtypeStruct((depth, nb, n6), F32),
        compiler_params=_cparams(("arbitrary", "arbitrary")),
        name="adaln_mods",
    )(c_all, ada_w, ada_b.reshape(depth, 1, n6))


def _mod_spec(n_ctx_tiles):
    return pl.BlockSpec((None, None, 6, D_MODEL), lambda b, j: (b, jnp.where(j >= n_ctx_tiles, 1, 0), 0, 0))


def _inproj_kernel(x_ref, mod_ref, g_ref, w_ref, qkv_ref, u_ref):
    h = _norm_mod(x_ref[0], g_ref[...], mod_ref[1:2, :], mod_ref[0:1, :]).astype(BF16)
    qkv_ref[0] = jnp.dot(h, w_ref[:, :N_QKV], preferred_element_type=F32).astype(BF16)
    u_ref[0] = jnp.dot(h, w_ref[:, N_QKV:], preferred_element_type=F32)


def _inproj(xs, modtab, g, w_in_bf, n_ctx_tiles):
    b, s, d = xs.shape
    nj = s // TM
    return pl.pallas_call(
        _inproj_kernel,
        grid=(b, nj),
        in_specs=[pl.BlockSpec((1, TM, d), lambda b, j: (b, j, 0)),
                  _mod_spec(n_ctx_tiles),
                  pl.BlockSpec((1, d), lambda b, j: (0, 0)),
                  pl.BlockSpec((d, N_QKV + D_SHIFT), lambda b, j: (0, 0))],
        out_specs=[pl.BlockSpec((1, TM, N_QKV), lambda b, j: (b, j, 0)),
                   pl.BlockSpec((1, TM, D_SHIFT), lambda b, j: (b, j, 0))],
        out_shape=[jax.ShapeDtypeStruct((b, s, N_QKV), BF16),
                   jax.ShapeDtypeStruct((b, s, D_SHIFT), F32)],
        compiler_params=_cparams(("arbitrary", "arbitrary")),
        name="in_proj",
    )(xs, modtab, g.reshape(1, d), w_in_bf)


def _na_bias_table(rpb):
    nh = rpb.shape[0]
    cq = np.arange(GRID_W)[:, None]
    ck = np.arange(GRID_W)[None, :]
    c0 = np.clip(cq - 8, 0, GRID_W - 16)
    col_ok = (ck >= c0) & (ck < c0 + 16)
    col_off = np.clip(ck - cq + 15, 0, 30)
    t = jnp.take(rpb, jnp.asarray(col_off.reshape(-1), jnp.int32), axis=2).reshape(nh, 15, GRID_W, GRID_W)
    t = jnp.where(col_ok[None, None], t, NEG)
    tabs = []
    for case in range(3):
        per_row = []
        for i in range(4):
            lo, ro = ((0, 7 - i), (i, 3), (4, 3 - i))[case]
            blk = jnp.pad(t[:, ro:ro + 8], ((0, 0), (lo, NA_WIN_ROWS - 8 - lo), (0, 0), (0, 0)),
                          constant_values=NEG)
            per_row.append(blk.transpose(0, 2, 1, 3))
        tabs.append(jnp.stack(per_row, axis=1).reshape(nh, TM, NA_WIN_ROWS * GRID_W))
    return jnp.stack(tabs, axis=1)


def _na_kernel(q_ref, k_ref, v_ref, bias_ref, o_ref, *, n_ctx, rows):
    j = pl.program_id(1)
    n_ctx_tiles = n_ctx // TM
    scale = HEAD_DIM ** -0.5
    nwin = NA_WIN_ROWS * GRID_W

    def head(h):
        return slice(h * HEAD_DIM, (h + 1) * HEAD_DIM)

    @pl.when(j < n_ctx_tiles)
    def _ctx():
        for h in range(H_NA):
            q = q_ref[0, :, head(h)]
            s = _dot_nt(q, k_ref[0, 0:n_ctx, head(h)]) * scale
            p = jnp.exp(s - jnp.max(s, axis=-1, keepdims=True))
            y = jnp.dot(p.astype(BF16), v_ref[0, 0:n_ctx, head(h)], preferred_element_type=F32)
            o_ref[0, :, head(h)] = (y / jnp.sum(p, axis=-1, keepdims=True)).astype(BF16)

    @pl.when(j >= n_ctx_tiles)
    def _lat():
        m = j - n_ctx_tiles
        w0 = jnp.clip(4 * m - 4, 0, rows - NA_WIN_ROWS)
        start = pl.multiple_of(n_ctx + w0 * GRID_W, GRID_W)
        for h in range(H_NA):
            q = q_ref[0, :, head(h)]
            s_loc = _dot_nt(q, k_ref[0, pl.ds(start, nwin), head(h)]) * scale + bias_ref[h]
            s_ctx = _dot_nt(q, k_ref[0, 0:n_ctx, head(h)]) * scale
            mx = jnp.maximum(jnp.max(s_loc, axis=-1, keepdims=True), jnp.max(s_ctx, axis=-1, keepdims=True))
            p_loc = jnp.exp(s_loc - mx)
            p_ctx = jnp.exp(s_ctx - mx)
            den = jnp.sum(p_loc, axis=-1, keepdims=True) + jnp.sum(p_ctx, axis=-1, keepdims=True)
            y = (jnp.dot(p_loc.astype(BF16), v_ref[0, pl.ds(start, nwin), head(h)], preferred_element_type=F32)
                 + jnp.dot(p_ctx.astype(BF16), v_ref[0, 0:n_ctx, head(h)], preferred_element_type=F32))
            o_ref[0, :, head(h)] = (y / den).astype(BF16)


def _na(qkv, bias_tab, n_ctx):
    b, s, _ = qkv.shape
    nj = s // TM
    n_ctx_tiles = n_ctx // TM
    rows = (s - n_ctx) // GRID_W
    n_lat_tiles = nj - n_ctx_tiles
    assert rows >= NA_WIN_ROWS and rows % 4 == 0 and n_lat_tiles >= 3

    def case_map(b, j):
        m = j - n_ctx_tiles
        return (0, jnp.where(m <= 0, 0, jnp.where(m >= n_lat_tiles - 1, 2, 1)), 0, 0)

    return pl.pallas_call(
        functools.partial(_na_kernel, n_ctx=n_ctx, rows=rows),
        grid=(b, nj),
        in_specs=[pl.BlockSpec((1, TM, D_NA), lambda b, j: (b, j, 0)),
                  pl.BlockSpec((1, s, D_NA), lambda b, j: (b, 0, 1)),
                  pl.BlockSpec((1, s, D_NA), lambda b, j: (b, 0, 2)),
                  pl.BlockSpec((H_NA, None, TM, NA_WIN_ROWS * GRID_W), case_map)],
        out_specs=pl.BlockSpec((1, TM, D_NA), lambda b, j: (b, j, 0)),
        out_shape=jax.ShapeDtypeStruct((b, s, D_NA), BF16),
        compiler_params=_cparams(("arbitrary", "arbitrary")),
        name="na_attention",
    )(qkv, qkv, qkv, bias_tab)


def _feat_kernel(u_ref, up_ref, un_ref, mu_ref, w0_ref, w2_ref, a0_ref, a2_ref, g2_ref, kk_ref, ka_ref, rk_ref,
                 bd_ref, feat_ref, rd_ref, *, n_ctx_tiles, nj):
    j = pl.program_id(1)
    seg_start = (j == 0) | (j == n_ctx_tiles)
    seg_end = (j == n_ctx_tiles - 1) | (j == nj - 1)
    rid = lax.broadcasted_iota(jnp.int32, (TM, 1), 0)

    def shifted(lo, hi):
        p = u_ref[0, :, lo:hi]
        prev_row = jnp.where(seg_start, 0.0, up_ref[0, 7:8, lo:hi])
        next_row = jnp.where(seg_end, 0.0, un_ref[0, 0:1, lo:hi])
        prev = jnp.where(rid == 0, prev_row, pltpu.roll(p, 1, axis=0))
        nxt = jnp.where(rid == TM - 1, next_row, pltpu.roll(p, TM - 1, axis=0))
        return p + mu_ref[:, lo:hi] * (0.5 * (prev + nxt) - p)

    c = D_RWKV
    r = shifted(0, c)
    k = shifted(c, 2 * c)
    v = shifted(2 * c, 3 * c)
    o = 3 * c
    wl = shifted(o, o + 2 * R_DECAY)
    al = shifted(o + 2 * R_DECAY, o + 2 * R_DECAY + 2 * R_AAA)
    gl = shifted(o + 2 * R_DECAY + 2 * R_AAA, D_SHIFT)
    bd = bd_ref[...]

    lw = jnp.dot(jnp.tanh(wl), w2_ref[...], precision=HI, preferred_element_type=F32)
    la = jnp.dot(al, a2_ref[...], precision=HI, preferred_element_type=F32)
    g = jnp.dot(_sigmoid(gl), g2_ref[...], precision=HI, preferred_element_type=F32)

    kks = k * kk_ref[...]
    kk = kks / jnp.maximum(jnp.sqrt(_head_sum(kks * kks, bd)), L2_EPS)
    feat_ref[0, :, 0:c] = r
    feat_ref[0, :, c:2 * c] = v
    feat_ref[0, :, 2 * c:3 * c] = kk
    ksum = jnp.zeros_like(k)
    for z in range(2):
        w_log = -_softplus(-(w0_ref[z:z + 1, :] + lw[:, z * c:(z + 1) * c])) - 0.5
        decay = jnp.exp(-jnp.exp(w_log))
        a = _sigmoid(a0_ref[z:z + 1, :] + la[:, z * c:(z + 1) * c])
        k_dir = k * (1.0 + (a - 1.0) * ka_ref[...])
        ksum = ksum + k_dir
        base = (3 + 3 * z) * c
        feat_ref[0, :, base:base + c] = decay
        feat_ref[0, :, base + c:base + 2 * c] = k_dir
        feat_ref[0, :, base + 2 * c:base + 3 * c] = kk * a
    bonus = _head_sum(r * (0.5 * ksum) * rk_ref[...], bd) * v
    rd_ref[0, :, 0:c] = bonus
    rd_ref[0, :, c:2 * c] = g


def _features(u_raw, n_ctx, shift_mu, w0, w2blk, a0, a2blk, g2, k_k, k_a, r_k, bd):
    b, s, _ = u_raw.shape
    nj = s // TM
    n_ctx_tiles = n_ctx // TM
    c = D_RWKV
    t8 = TM // 8

    def full(shape):
        return pl.BlockSpec(shape, lambda b, j: (0,) * len(shape))

    return pl.pallas_call(
        functools.partial(_feat_kernel, n_ctx_tiles=n_ctx_tiles, nj=nj),
        grid=(b, nj),
        in_specs=[pl.BlockSpec((1, TM, D_SHIFT), lambda b, j: (b, j, 0)),
                  pl.BlockSpec((1, 8, D_SHIFT), lambda b, j: (b, jnp.maximum(j * t8 - 1, 0), 0)),
                  pl.BlockSpec((1, 8, D_SHIFT), lambda b, j: (b, jnp.minimum((j + 1) * t8, s // 8 - 1), 0)),
                  full((1, D_SHIFT)), full((2, c)), full((2 * R_DECAY, 2 * c)), full((2, c)),
                  full((2 * R_AAA, 2 * c)), full((R_GATE, c)), full((1, c)), full((1, c)), full((1, c)),
                  full((c, c))],
        out_specs=[pl.BlockSpec((1, TM, N_FEAT * c), lambda b, j: (b, j, 0)),
                   pl.BlockSpec((1, TM, 2 * c), lambda b, j: (b, j, 0))],
        out_shape=[jax.ShapeDtypeStruct((b, s, N_FEAT * c), F32),
                   jax.ShapeDtypeStruct((b, s, 2 * c), F32)],
        compiler_params=_cparams(("arbitrary", "arbitrary")),
        name="rwkv_features",
    )(u_raw, u_raw, u_raw, shift_mu.reshape(1, D_SHIFT), w0, w2blk, a0, a2blk, g2, k_k.reshape(1, c),
      k_a.reshape(1, c), r_k.reshape(1, c), bd)


def _scan_kernel(fs_ref, fd_ref, bs_ref, bd_ref, yf_ref, yb_ref, st_ref):
    @pl.when(pl.program_id(0) == 0)
    def _init():
        st_ref[...] = jnp.zeros_like(st_ref)

    def one_step(d, s_ref, d_ref, y_ref, row):
        v_t = s_ref[row, 1]
        sa = jnp.zeros_like(v_t)
        for k in range(HEAD_DIM):
            sa = sa + st_ref[d, k] * s_ref[row, 2, pl.ds(k, 1), :]
        y = jnp.zeros_like(v_t)
        for k in range(HEAD_DIM):
            s_new = (st_ref[d, k] * d_ref[row, 0, pl.ds(k, 1), :] - sa * d_ref[row, 2, pl.ds(k, 1), :]
                     + v_t * d_ref[row, 1, pl.ds(k, 1), :])
            st_ref[d, k] = s_new
            y = y + s_new * s_ref[row, 0, pl.ds(k, 1), :]
        y_ref[row] = y

    def step(jj, carry):
        one_step(0, fs_ref, fd_ref, yf_ref, jj)
        one_step(1, bs_ref, bd_ref, yb_ref, TC_SCAN - 1 - jj)
        return carry

    lax.fori_loop(0, TC_SCAN, step, 0)


def _scan(feat_t, n_ctx):
    s, _, hd, lanes = feat_t.shape
    nblk = s // TC_SCAN
    nc = n_ctx // TC_SCAN
    assert n_ctx % TC_SCAN == 0 and s % TC_SCAN == 0

    def bwd(i):
        return jnp.where(i < nc, nc - 1 - i, nblk - 1 - i + nc)

    blk = (TC_SCAN, 3, hd, lanes)
    yblk = (TC_SCAN, hd, lanes)
    return pl.pallas_call(
        _scan_kernel,
        grid=(nblk,),
        in_specs=[pl.BlockSpec(blk, lambda i: (i, 0, 0, 0)),
                  pl.BlockSpec(blk, lambda i: (i, 1, 0, 0)),
                  pl.BlockSpec(blk, lambda i: (bwd(i), 0, 0, 0)),
                  pl.BlockSpec(blk, lambda i: (bwd(i), 2, 0, 0))],
        out_specs=[pl.BlockSpec(yblk, lambda i: (i, 0, 0)),
                   pl.BlockSpec(yblk, lambda i: (bwd(i), 0, 0))],
        out_shape=[jax.ShapeDtypeStruct((s, hd, lanes), F32)] * 2,
        scratch_shapes=[pltpu.VMEM((2, HEAD_DIM, hd, lanes), F32)],
        compiler_params=_cparams(("arbitrary",)),
        name="rwkv_scan",
    )(feat_t, feat_t, feat_t, feat_t)


def _outproj_kernel(yf_ref, yb_ref, rd_ref, yna_ref, x_ref, mod_ref, lnw_ref, lnb_ref, bd_ref, w_ref, o_ref):
    c = D_RWKV
    bd = bd_ref[...]
    y = yf_ref[0] + yb_ref[0]
    mu = _head_sum(y, bd) * (1.0 / HEAD_DIM)
    dlt = y - mu
    var = _head_sum(dlt * dlt, bd) * (1.0 / HEAD_DIM)
    yn = dlt * lax.rsqrt(var + GN_EPS) * lnw_ref[...] + lnb_ref[...]
    y_rw = ((yn + rd_ref[0, :, 0:c]) * rd_ref[0, :, c:2 * c]).astype(BF16)
    o = (jnp.dot(y_rw, w_ref[0:c, :], preferred_element_type=F32)
         + jnp.dot(yna_ref[0], w_ref[c:, :], preferred_element_type=F32))
    o_ref[0] = x_ref[0] + mod_ref[2:3, :] * o


def _outproj(yf, yb, rd, yna, xs, modtab, ln_w, ln_b, bd, w_out_bf, n_ctx_tiles):
    b, s, d = xs.shape
    nj = s // TM
    c = D_RWKV

    def tile(w):
        return pl.BlockSpec((1, TM, w), lambda b, j: (b, j, 0))

    def full(shape):
        return pl.BlockSpec(shape, lambda b, j: (0,) * len(shape))

    return pl.pallas_call(
        _outproj_kernel,
        grid=(b, nj),
        in_specs=[tile(c), tile(c), tile(2 * c), tile(D_NA), tile(d), _mod_spec(n_ctx_tiles),
                  full((1, c)), full((1, c)), full((c, c)), full((d, d))],
        out_specs=tile(d),
        out_shape=jax.ShapeDtypeStruct((b, s, d), F32),
        compiler_params=_cparams(("arbitrary", "arbitrary")),
        name="out_proj",
    )(yf, yb, rd, yna, xs, modtab, ln_w.reshape(1, c), ln_b.reshape(1, c), bd, w_out_bf)


def _ffn_kernel(x_ref, mod_ref, g_ref, w1_ref, w3_ref, w2_ref, o_ref, *, ffc):
    x = x_ref[0]
    h = _norm_mod(x, g_ref[...], mod_ref[4:5, :], mod_ref[3:4, :]).astype(BF16)
    d_ff = w1_ref.shape[1]
    acc = jnp.zeros(x.shape, F32)
    for c0 in range(0, d_ff, ffc):
        a = jnp.dot(h, w1_ref[:, c0:c0 + ffc], preferred_element_type=F32)
        g = jnp.dot(h, w3_ref[:, c0:c0 + ffc], preferred_element_type=F32)
        acc = acc + jnp.dot((_silu(a) * g).astype(BF16), w2_ref[c0:c0 + ffc, :], preferred_element_type=F32)
    o_ref[0] = x + mod_ref[5:6, :] * acc


def _ffn(xs, modtab, g, w1, w3, w2, n_ctx_tiles):
    b, s, d = xs.shape
    nj = s // TM
    d_ff = w1.shape[1]
    const = lambda b, j: (0, 0)
    return pl.pallas_call(
        functools.partial(_ffn_kernel, ffc=d_ff // 2),
        grid=(b, nj),
        in_specs=[pl.BlockSpec((1, TM, d), lambda b, j: (b, j, 0)), _mod_spec(n_ctx_tiles),
                  pl.BlockSpec((1, d), const),
                  pl.BlockSpec((d, d_ff), const), pl.BlockSpec((d, d_ff), const), pl.BlockSpec((d_ff, d), const)],
        out_specs=pl.BlockSpec((1, TM, d), lambda b, j: (b, j, 0)),
        out_shape=jax.ShapeDtypeStruct((b, s, d), F32),
        compiler_params=_cparams(("arbitrary", "arbitrary"), 56),
        name="ffn_swiglu",
    )(xs, modtab, g.reshape(1, d), w1, w3, w2)


def _router_kernel(x_ref, mod_ref, g_ref, r_ref, h_ref, slab_ref):
    h = _norm_mod(x_ref[0], g_ref[...], mod_ref[4:5, :], mod_ref[3:4, :])
    h_ref[0] = h
    logits = jnp.dot(h, r_ref[...], precision=HI, preferred_element_type=F32)
    lane = lax.broadcasted_iota(jnp.int32, logits.shape, 1)
    lg = jnp.where(lane < N_EXPERTS, logits, -jnp.inf)
    m1 = jnp.max(lg, axis=-1, keepdims=True)
    i1 = jnp.min(jnp.where(lg == m1, lane, LANES), axis=-1, keepdims=True)
    lg2 = jnp.where(lane == i1, -jnp.inf, lg)
    m2 = jnp.max(lg2, axis=-1, keepdims=True)
    i2 = jnp.min(jnp.where(lg2 == m2, lane, LANES), axis=-1, keepdims=True)
    e = jnp.exp(m2 - m1)
    g1 = 1.0 / (1.0 + e)
    g2 = e / (1.0 + e)
    slab_ref[0] = jnp.where(lane == 0, i1.astype(F32),
                            jnp.where(lane == 1, i2.astype(F32),
                                      jnp.where(lane == 2, g1, jnp.where(lane == 3, g2, 0.0))))


def _router(xs, modtab, g, router_pad, n_ctx_tiles):
    b, s, d = xs.shape
    nj = s // TM
    const = lambda b, j: (0, 0)
    return pl.pallas_call(
        _router_kernel,
        grid=(b, nj),
        in_specs=[pl.BlockSpec((1, TM, d), lambda b, j: (b, j, 0)), _mod_spec(n_ctx_tiles),
                  pl.BlockSpec((1, d), const), pl.BlockSpec((d, LANES), const)],
        out_specs=[pl.BlockSpec((1, TM, d), lambda b, j: (b, j, 0)),
                   pl.BlockSpec((1, TM, LANES), lambda b, j: (b, j, 0))],
        out_shape=[jax.ShapeDtypeStruct((b, s, d), F32), jax.ShapeDtypeStruct((b, s, LANES), F32)],
        compiler_params=_cparams(("arbitrary", "arbitrary")),
        name="moe_router",
    )(xs, modtab, g.reshape(1, d), router_pad)


def _dispatch(idx):
    n = idx.shape[0]
    a = n * 2
    flat_e = idx.reshape(a)
    onehot = (flat_e[None, :] == jnp.arange(N_EXPERTS, dtype=jnp.int32)[:, None]).astype(jnp.int32)
    csum = jnp.cumsum(onehot, axis=1)
    counts = csum[:, -1]
    rank = jnp.sum(csum * onehot, axis=0) - 1
    padded = (counts + MOE_BLOCK - 1) // MOE_BLOCK * MOE_BLOCK
    pad_ends = jnp.cumsum(padded)
    pad_starts = pad_ends - padded
    dest = (pad_starts[flat_e] + rank).astype(jnp.int32)
    n_blocks = -(-a // MOE_BLOCK) + N_EXPERTS
    slot_tok = jnp.zeros((n_blocks * MOE_BLOCK,), jnp.int32).at[dest].set(jnp.arange(a, dtype=jnp.int32) // 2)
    block_start = jnp.arange(n_blocks, dtype=jnp.int32) * MOE_BLOCK
    block_e = jnp.minimum(jnp.sum((pad_ends[None, :] <= block_start[:, None]).astype(jnp.int32), axis=1),
                          N_EXPERTS - 1)
    n_used = (pad_ends[-1] // MOE_BLOCK).astype(jnp.int32).reshape(1)
    return slot_tok, block_e, n_used, dest


def _moe_mm_kernel(be_ref, st_ref, nu_ref, h_hbm, w1_hbm, w3_hbm, w2_hbm, o_ref, xbuf, w1s, w3s, w2s, gsem, wsem,
                   *, ffc):
    i = pl.program_id(0)
    n_used = nu_ref[0]

    def row_copy(tok, slot, r):
        return pltpu.make_async_copy(h_hbm.at[pl.ds(tok, 1)], xbuf.at[slot, pl.ds(r, 1)], gsem.at[slot])

    def issue(blk, slot):
        def body(r, carry):
            row_copy(st_ref[blk * MOE_BLOCK + r], slot, r).start()
            return carry
        lax.fori_loop(0, MOE_BLOCK, body, 0)

    def wait_rows(slot):
        def body(r, carry):
            row_copy(0, slot, r).wait()
            return carry
        lax.fori_loop(0, MOE_BLOCK, body, 0)

    @pl.when(i == 0)
    def _first():
        issue(0, 0)

    @pl.when(i + 1 < n_used)
    def _prefetch():
        issue(i + 1, lax.rem(i + 1, 2))

    @pl.when(i < n_used)
    def _compute():
        e = be_ref[i]
        changed = (i == 0) | (e != be_ref[jnp.maximum(i - 1, 0)])

        @pl.when(changed)
        def _load_weights():
            copies = [pltpu.make_async_copy(w1_hbm.at[e], w1s, wsem.at[0]),
                      pltpu.make_async_copy(w3_hbm.at[e], w3s, wsem.at[1]),
                      pltpu.make_async_copy(w2_hbm.at[e], w2s, wsem.at[2])]
            for cp in copies:
                cp.start()
            for cp in copies:
                cp.wait()

        slot = lax.rem(i, 2)
        wait_rows(slot)
        xb = xbuf[slot].astype(BF16)
        acc = jnp.zeros(o_ref.shape, F32)
        for c0 in range(0, w1s.shape[1], ffc):
            a = jnp.dot(xb, w1s[:, c0:c0 + ffc], preferred_element_type=F32)
            g = jnp.dot(xb, w3s[:, c0:c0 + ffc], preferred_element_type=F32)
            acc = acc + jnp.dot((_silu(a) * g).astype(BF16), w2s[c0:c0 + ffc, :], preferred_element_type=F32)
        o_ref[...] = acc

    @pl.when(i >= n_used)
    def _unused():
        o_ref[...] = jnp.zeros_like(o_ref)


def _moe_mm(h_flat, slot_tok, block_e, n_used, w1, w3, w2):
    n, d = h_flat.shape
    n_blocks = block_e.shape[0]
    d_ff = w1.shape[2]
    any_spec = pl.BlockSpec(memory_space=pl.ANY)
    return pl.pallas_call(
        functools.partial(_moe_mm_kernel, ffc=512),
        grid_spec=pltpu.PrefetchScalarGridSpec(
            num_scalar_prefetch=3,
            grid=(n_blocks,),
            in_specs=[any_spec, any_spec, any_spec, any_spec],
            out_specs=pl.BlockSpec((MOE_BLOCK, d), lambda i, be, st, nu: (i, 0)),
            scratch_shapes=[pltpu.VMEM((2, MOE_BLOCK, d), F32),
                            pltpu.VMEM((d, d_ff), BF16), pltpu.VMEM((d, d_ff), BF16), pltpu.VMEM((d_ff, d), BF16),
                            pltpu.SemaphoreType.DMA((2,)), pltpu.SemaphoreType.DMA((3,))]),
        out_shape=jax.ShapeDtypeStruct((n_blocks * MOE_BLOCK, d), F32),
        compiler_params=_cparams(("arbitrary",), 56),
        name="moe_grouped_swiglu",
    )(block_e, slot_tok, n_used, h_flat, w1, w3, w2)


def _moe_comb_kernel(inv_ref, x_ref, mod_ref, slab_ref, fg_ref, yb_hbm, o_ref, gbuf, gsem, *, nj, final):
    b = pl.program_id(0)
    j = pl.program_id(1)
    t = b * nj + j
    nt = pl.num_programs(0) * nj

    def row_copy(src, slot, which, r):
        return pltpu.make_async_copy(yb_hbm.at[pl.ds(src, 1)], gbuf.at[slot, which, pl.ds(r, 1)], gsem.at[slot])

    def issue(tile, slot):
        def body(r, carry):
            base = (tile * TM + r) * 2
            row_copy(inv_ref[base], slot, 0, r).start()
            row_copy(inv_ref[base + 1], slot, 1, r).start()
            return carry
        lax.fori_loop(0, TM, body, 0)

    def wait_rows(slot):
        def body(r, carry):
            row_copy(0, slot, 0, r).wait()
            row_copy(0, slot, 1, r).wait()
            return carry
        lax.fori_loop(0, TM, body, 0)

    @pl.when(t == 0)
    def _first():
        issue(0, 0)

    @pl.when(t + 1 < nt)
    def _prefetch():
        issue(t + 1, lax.rem(t + 1, 2))

    slot = lax.rem(t, 2)
    wait_rows(slot)
    slab = slab_ref[0]
    y = slab[:, 2:3] * gbuf[slot, 0] + slab[:, 3:4] * gbuf[slot, 1]
    xn = x_ref[0] + mod_ref[5:6, :] * y
    if final:
        xn = _rms(xn) * fg_ref[...]
    o_ref[0] = xn


def _moe_combine(inv, xs, modtab, slab, final_g, yb, n_ctx_tiles, final):
    b, s, d = xs.shape
    nj = s // TM
    return pl.pallas_call(
        functools.partial(_moe_comb_kernel, nj=nj, final=final),
        grid_spec=pltpu.PrefetchScalarGridSpec(
            num_scalar_prefetch=1,
            grid=(b, nj),
            in_specs=[pl.BlockSpec((1, TM, d), lambda b, j, inv: (b, j, 0)),
                      pl.BlockSpec((None, None, 6, D_MODEL),
                                   lambda b, j, inv: (b, jnp.where(j >= n_ctx_tiles, 1, 0), 0, 0)),
                      pl.BlockSpec((1, TM, LANES), lambda b, j, inv: (b, j, 0)),
                      pl.BlockSpec((1, d), lambda b, j, inv: (0, 0)),
                      pl.BlockSpec(memory_space=pl.ANY)],
            out_specs=pl.BlockSpec((1, TM, d), lambda b, j, inv: (b, j, 0)),
            scratch_shapes=[pltpu.VMEM((2, 2, TM, d), F32), pltpu.SemaphoreType.DMA((2,))]),
        out_shape=jax.ShapeDtypeStruct((b, s, d), F32),
        compiler_params=_cparams(("arbitrary", "arbitrary")),
        name="moe_combine",
    )(inv, xs, modtab, slab, final_g.reshape(1, d), yb)


def _blockdiag(w):
    z = jnp.zeros_like(w[0])
    return jnp.concatenate([jnp.concatenate([w[0], z], axis=1), jnp.concatenate([z, w[1]], axis=1)], axis=0)


def kernel(x, c, ctx, c_ctx, ada_w, ada_b, norm_mix_g, norm_ffn_g, w_in, shift_mu, w0, w2, a0, a2, g2, k_k, k_a, r_k,
           ln_x_w, ln_x_b, na_rpb, w_out, ffn_w1, ffn_w3, ffn_w2, router, moe_w1, moe_w3, moe_w2, final_g):
    b, t, d = x.shape
    n_ctx = ctx.shape[1]
    s = n_ctx + t
    depth = ada_w.shape[0]
    assert d == D_MODEL and n_ctx % TM == 0 and t % TM == 0
    n_ctx_tiles = n_ctx // TM
    lanes = b * H_RWKV

    nb = -(-(b + 1) // 8) * 8
    c_all = jnp.concatenate([c, c_ctx[None, :], jnp.zeros((nb - b - 1, d), F32)], axis=0)
    mods = _mods(c_all, ada_w, ada_b).reshape(depth, nb, 6, d)
    modtab = jnp.stack([jnp.broadcast_to(mods[:, b:b + 1], (depth, b, 6, d)), mods[:, :b]], axis=2)

    head_id = np.arange(D_RWKV) // HEAD_DIM
    bd = jnp.asarray(head_id[:, None] == head_id[None, :], BF16)

    xs = jnp.concatenate([ctx, x], axis=1)
    for l in range(depth):
        last = l == depth - 1
        qkv, u_raw = _inproj(xs, modtab[l], norm_mix_g[l], w_in[l].astype(BF16), n_ctx_tiles)
        y_na = _na(qkv, _na_bias_table(na_rpb[l]), n_ctx)
        feat, rd = _features(u_raw, n_ctx, shift_mu[l], w0[l], _blockdiag(w2[l]), a0[l], _blockdiag(a2[l]), g2[l],
                             k_k[l], k_a[l], r_k[l], bd)
        feat_t = feat.reshape(b, s, N_FEAT, H_RWKV, HEAD_DIM).transpose(1, 2, 4, 0, 3).reshape(
            s, N_FEAT, HEAD_DIM, lanes)
        yf_t, yb_t = _scan(feat_t, n_ctx)

        def untranspose(y):
            return y.reshape(s, HEAD_DIM, b, H_RWKV).transpose(2, 0, 3, 1).reshape(b, s, D_RWKV)

        xs = _outproj(untranspose(yf_t), untranspose(yb_t), rd, y_na, xs, modtab[l], ln_x_w[l], ln_x_b[l], bd,
                      w_out[l].astype(BF16), n_ctx_tiles)
        i = l // 2
        if l % 2 == 0:
            xs = _ffn(xs, modtab[l], norm_ffn_g[l], ffn_w1[i].astype(BF16), ffn_w3[i].astype(BF16),
                      ffn_w2[i].astype(BF16), n_ctx_tiles)
        else:
            router_pad = jnp.pad(router[i], ((0, 0), (0, LANES - N_EXPERTS)))
            h, slab = _router(xs, modtab[l], norm_ffn_g[l], router_pad, n_ctx_tiles)
            idx = slab[..., 0:2].astype(jnp.int32).reshape(b * s, 2)
            slot_tok, block_e, n_used, dest = _dispatch(idx)
            yb = _moe_mm(h.reshape(b * s, d), slot_tok, block_e, n_used, moe_w1[i].astype(BF16),
                         moe_w3[i].astype(BF16), moe_w2[i].astype(BF16))
            xs = _moe_combine(dest, xs, modtab[l], slab, final_g, yb, n_ctx_tiles, final=last)
    if depth % 2 == 1:
        raise NotImplementedError("final norm is fused into the MoE combine of the last (odd) layer")
    return xs[:, n_ctx:, :]
```

```python
import functools

import numpy as np
import jax
import jax.numpy as jnp
from jax import lax
from jax.experimental import pallas as pl
from jax.experimental.pallas import tpu as pltpu

F32 = jnp.float32
BF16 = jnp.bfloat16
HI = lax.Precision.HIGHEST

D_MODEL = 1024
HEAD_DIM = 64
GRID_W = 64
D_RWKV = 512
D_NA = 512
H_RWKV = D_RWKV // HEAD_DIM
H_NA = D_NA // HEAD_DIM
R_DECAY = 64
R_AAA = 64
R_GATE = 128
D_SHIFT = 3 * D_RWKV + 2 * R_DECAY + 2 * R_AAA + R_GATE
N_QKV = 3 * D_NA
N_EXPERTS = 8
MOE_BLOCK = 256
RMS_EPS = 1e-6
GN_EPS = 64e-5
L2_EPS = 1e-12
NEG = -1e30

TM = 256
NA_WIN_ROWS = 12
N_FEAT = 9
TC_SCAN = 16
LANES = 128


def _cparams(sem, vmem_mb=48):
    return pltpu.CompilerParams(dimension_semantics=sem, vmem_limit_bytes=vmem_mb * 1024 * 1024)


def _sigmoid(x):
    return 1.0 / (1.0 + jnp.exp(-x))


def _silu(x):
    return x * _sigmoid(x)


def _softplus(x):
    return jnp.maximum(x, 0.0) + jnp.log(1.0 + jnp.exp(-jnp.abs(x)))


def _rms(x):
    return x * lax.rsqrt(jnp.mean(x * x, axis=-1, keepdims=True) + RMS_EPS)


def _norm_mod(x, g, scale, shift):
    return _rms(x) * g * (1.0 + scale) + shift


def _head_sum(x, bd):
    hi = x.astype(BF16)
    lo = (x - hi.astype(F32)).astype(BF16)
    return jnp.dot(hi, bd, preferred_element_type=F32) + jnp.dot(lo, bd, preferred_element_type=F32)


def _dot_nt(a, b):
    return lax.dot_general(a, b, (((1,), (1,)), ((), ())), preferred_element_type=F32)


def _mods_kernel(c_ref, w_ref, b_ref, o_ref):
    o_ref[0] = jnp.dot(_silu(c_ref[...]), w_ref[0], precision=HI, preferred_element_type=F32) + b_ref[0]


def _mods(c_all, ada_w, ada_b):
    depth, d, n6 = ada_w.shape
    nb = c_all.shape[0]
    tn = 1536
    return pl.pallas_call(
        _mods_kernel,
        grid=(depth, n6 // tn),
        in_specs=[pl.BlockSpec((nb, d), lambda l, n: (0, 0)),
                  pl.BlockSpec((1, d, tn), lambda l, n: (l, 0, n)),
                  pl.BlockSpec((1, 1, tn), lambda l, n: (l, 0, n))],
        out_specs=pl.BlockSpec((1, nb, tn), lambda l, n: (l, 0, n)),
        out_shape=jax.ShapeDtypeStruct((depth, nb, n6), F32),
        compiler_params=_cparams(("arbitrary", "arbitrary")),
        name="adaln_mods",
    )(c_all, ada_w, ada_b.reshape(depth, 1, n6))


def _mod_spec(n_ctx_tiles):
    return pl.BlockSpec((None, None, 6, D_MODEL), lambda b, j: (b, jnp.where(j >= n_ctx_tiles, 1, 0), 0, 0))


def _inproj_kernel(x_ref, mod_ref, g_ref, w_ref, qkv_ref, u_ref):
    h = _norm_mod(x_ref[0], g_ref[...], mod_ref[1:2, :], mod_ref[0:1, :]).astype(BF16)
    qkv_ref[0] = jnp.dot(h, w_ref[:, :N_QKV], preferred_element_type=F32).astype(BF16)
    u_ref[0] = jnp.dot(h, w_ref[:, N_QKV:], preferred_element_type=F32)


def _inproj(xs, modtab, g, w_in_bf, n_ctx_tiles):
    b, s, d = xs.shape
    nj = s // TM
    return pl.pallas_call(
        _inproj_kernel,
        grid=(b, nj),
        in_specs=[pl.BlockSpec((1, TM, d), lambda b, j: (b, j, 0)),
                  _mod_spec(n_ctx_tiles),
                  pl.BlockSpec((1, d), lambda b, j: (0, 0)),
                  pl.BlockSpec((d, N_QKV + D_SHIFT), lambda b, j: (0, 0))],
        out_specs=[pl.BlockSpec((1, TM, N_QKV), lambda b, j: (b, j, 0)),
                   pl.BlockSpec((1, TM, D_SHIFT), lambda b, j: (b, j, 0))],
        out_shape=[jax.ShapeDtypeStruct((b, s, N_QKV), BF16),
                   jax.ShapeDtypeStruct((b, s, D_SHIFT), F32)],
        compiler_params=_cparams(("arbitrary", "arbitrary")),
        name="in_proj",
    )(xs, modtab, g.reshape(1, d), w_in_bf)


def _na_bias_table(rpb):
    nh = rpb.shape[0]
    cq = np.arange(GRID_W)[:, None]
    ck = np.arange(GRID_W)[None, :]
    c0 = np.clip(cq - 8, 0, GRID_W - 16)
    col_ok = (ck >= c0) & (ck < c0 + 16)
    col_off = np.clip(ck - cq + 15, 0, 30)
    t = jnp.take(rpb, jnp.asarray(col_off.reshape(-1), jnp.int32), axis=2).reshape(nh, 15, GRID_W, GRID_W)
    t = jnp.where(col_ok[None, None], t, NEG)
    tabs = []
    for case in range(3):
        per_row = []
        for i in range(4):
            lo, ro = ((0, 7 - i), (i, 3), (4, 3 - i))[case]
            blk = jnp.pad(t[:, ro:ro + 8], ((0, 0), (lo, NA_WIN_ROWS - 8 - lo), (0, 0), (0, 0)),
                          constant_values=NEG)
            per_row.append(blk.transpose(0, 2, 1, 3))
        tabs.append(jnp.stack(per_row, axis=1).reshape(nh, TM, NA_WIN_ROWS * GRID_W))
    return jnp.stack(tabs, axis=1)


def _na_kernel(q_ref, k_ref, v_ref, bias_ref, o_ref, *, n_ctx, rows):
    j = pl.program_id(1)
    n_ctx_tiles = n_ctx // TM
    scale = HEAD_DIM ** -0.5
    nwin = NA_WIN_ROWS * GRID_W

    def head(h):
        return slice(h * HEAD_DIM, (h + 1) * HEAD_DIM)

    @pl.when(j < n_ctx_tiles)
    def _ctx():
        for h in range(H_NA):
            q = q_ref[0, :, head(h)]
            s = _dot_nt(q, k_ref[0, 0:n_ctx, head(h)]) * scale
            p = jnp.exp(s - jnp.max(s, axis=-1, keepdims=True))
            y = jnp.dot(p.astype(BF16), v_ref[0, 0:n_ctx, head(h)], preferred_element_type=F32)
            o_ref[0, :, head(h)] = (y / jnp.sum(p, axis=-1, keepdims=True)).astype(BF16)

    @pl.when(j >= n_ctx_tiles)
    def _lat():
        m = j - n_ctx_tiles
        w0 = jnp.clip(4 * m - 4, 0, rows - NA_WIN_ROWS)
        start = pl.multiple_of(n_ctx + w0 * GRID_W, GRID_W)
        for h in range(H_NA):
            q = q_ref[0, :, head(h)]
            s_loc = _dot_nt(q, k_ref[0, pl.ds(start, nwin), head(h)]) * scale + bias_ref[h]
            s_ctx = _dot_nt(q, k_ref[0, 0:n_ctx, head(h)]) * scale
            mx = jnp.maximum(jnp.max(s_loc, axis=-1, keepdims=True), jnp.max(s_ctx, axis=-1, keepdims=True))
            p_loc = jnp.exp(s_loc - mx)
            p_ctx = jnp.exp(s_ctx - mx)
            den = jnp.sum(p_loc, axis=-1, keepdims=True) + jnp.sum(p_ctx, axis=-1, keepdims=True)
            y = (jnp.dot(p_loc.astype(BF16), v_ref[0, pl.ds(start, nwin), head(h)], preferred_element_type=F32)
                 + jnp.dot(p_ctx.astype(BF16), v_ref[0, 0:n_ctx, head(h)], preferred_element_type=F32))
            o_ref[0, :, head(h)] = (y / den).astype(BF16)


def _na(qkv, bias_tab, n_ctx):
    b, s, _ = qkv.shape
    nj = s // TM
    n_ctx_tiles = n_ctx // TM
    rows = (s - n_ctx) // GRID_W
    n_lat_tiles = nj - n_ctx_tiles
    assert rows >= NA_WIN_ROWS and rows % 4 == 0 and n_lat_tiles >= 3

    def case_map(b, j):
        m = j - n_ctx_tiles
        return (0, jnp.where(m <= 0, 0, jnp.where(m >= n_lat_tiles - 1, 2, 1)), 0, 0)

    return pl.pallas_call(
        functools.partial(_na_kernel, n_ctx=n_ctx, rows=rows),
        grid=(b, nj),
        in_specs=[pl.BlockSpec((1, TM, D_NA), lambda b, j: (b, j, 0)),
                  pl.BlockSpec((1, s, D_NA), lambda b, j: (b, 0, 1)),
                  pl.BlockSpec((1, s, D_NA), lambda b, j: (b, 0, 2)),
                  pl.BlockSpec((H_NA, None, TM, NA_WIN_ROWS * GRID_W), case_map)],
        out_specs=pl.BlockSpec((1, TM, D_NA), lambda b, j: (b, j, 0)),
        out_shape=jax.ShapeDtypeStruct((b, s, D_NA), BF16),
        compiler_params=_cparams(("arbitrary", "arbitrary")),
        name="na_attention",
    )(qkv, qkv, qkv, bias_tab)


def _feat_kernel(u_ref, up_ref, un_ref, mu_ref, w0_ref, w2_ref, a0_ref, a2_ref, g2_ref, kk_ref, ka_ref, rk_ref,
                 bd_ref, feat_ref, rd_ref, *, n_ctx_tiles, nj):
    j = pl.program_id(1)
    seg_start = (j == 0) | (j == n_ctx_tiles)
    seg_end = (j == n_ctx_tiles - 1) | (j == nj - 1)
    rid = lax.broadcasted_iota(jnp.int32, (TM, 1), 0)

    def shifted(lo, hi):
        p = u_ref[0, :, lo:hi]
        prev_row = jnp.where(seg_start, 0.0, up_ref[0, 7:8, lo:hi])
        next_row = jnp.where(seg_end, 0.0, un_ref[0, 0:1, lo:hi])
        prev = jnp.where(rid == 0, prev_row, pltpu.roll(p, 1, axis=0))
        nxt = jnp.where(rid == TM - 1, next_row, pltpu.roll(p, TM - 1, axis=0))
        return p + mu_ref[:, lo:hi] * (0.5 * (prev + nxt) - p)

    c = D_RWKV
    r = shifted(0, c)
    k = shifted(c, 2 * c)
    v = shifted(2 * c, 3 * c)
    o = 3 * c
    wl = shifted(o, o + 2 * R_DECAY)
    al = shifted(o + 2 * R_DECAY, o + 2 * R_DECAY + 2 * R_AAA)
    gl = shifted(o + 2 * R_DECAY + 2 * R_AAA, D_SHIFT)
    bd = bd_ref[...]

    lw = jnp.dot(jnp.tanh(wl), w2_ref[...], precision=HI, preferred_element_type=F32)
    la = jnp.dot(al, a2_ref[...], precision=HI, preferred_element_type=F32)
    g = jnp.dot(_sigmoid(gl), g2_ref[...], precision=HI, preferred_element_type=F32)

    kks = k * kk_ref[...]
    kk = kks / jnp.maximum(jnp.sqrt(_head_sum(kks * kks, bd)), L2_EPS)
    feat_ref[0, :, 0:c] = r
    feat_ref[0, :, c:2 * c] = v
    feat_ref[0, :, 2 * c:3 * c] = kk
    ksum = jnp.zeros_like(k)
    for z in range(2):
        w_log = -_softplus(-(w0_ref[z:z + 1, :] + lw[:, z * c:(z + 1) * c])) - 0.5
        decay = jnp.exp(-jnp.exp(w_log))
        a = _sigmoid(a0_ref[z:z + 1, :] + la[:, z * c:(z + 1) * c])
        k_dir = k * (1.0 + (a - 1.0) * ka_ref[...])
        ksum = ksum + k_dir
        base = (3 + 3 * z) * c
        feat_ref[0, :, base:base + c] = decay
        feat_ref[0, :, base + c:base + 2 * c] = k_dir
        feat_ref[0, :, base + 2 * c:base + 3 * c] = kk * a
    bonus = _head_sum(r * (0.5 * ksum) * rk_ref[...], bd) * v
    rd_ref[0, :, 0:c] = bonus
    rd_ref[0, :, c:2 * c] = g


def _features(u_raw, n_ctx, shift_mu, w0, w2blk, a0, a2blk, g2, k_k, k_a, r_k, bd):
    b, s, _ = u_raw.shape
    nj = s // TM
    n_ctx_tiles = n_ctx // TM
    c = D_RWKV
    t8 = TM // 8

    def full(shape):
        return pl.BlockSpec(shape, lambda b, j: (0,) * len(shape))

    return pl.pallas_call(
        functools.partial(_feat_kernel, n_ctx_tiles=n_ctx_tiles, nj=nj),
        grid=(b, nj),
        in_specs=[pl.BlockSpec((1, TM, D_SHIFT), lambda b, j: (b, j, 0)),
                  pl.BlockSpec((1, 8, D_SHIFT), lambda b, j: (b, jnp.maximum(j * t8 - 1, 0), 0)),
                  pl.BlockSpec((1, 8, D_SHIFT), lambda b, j: (b, jnp.minimum((j + 1) * t8, s // 8 - 1), 0)),
                  full((1, D_SHIFT)), full((2, c)), full((2 * R_DECAY, 2 * c)), full((2, c)),
                  full((2 * R_AAA, 2 * c)), full((R_GATE, c)), full((1, c)), full((1, c)), full((1, c)),
                  full((c, c))],
        out_specs=[pl.BlockSpec((1, TM, N_FEAT * c), lambda b, j: (b, j, 0)),
                   pl.BlockSpec((1, TM, 2 * c), lambda b, j: (b, j, 0))],
        out_shape=[jax.ShapeDtypeStruct((b, s, N_FEAT * c), F32),
                   jax.ShapeDtypeStruct((b, s, 2 * c), F32)],
        compiler_params=_cparams(("arbitrary", "arbitrary")),
        name="rwkv_features",
    )(u_raw, u_raw, u_raw, shift_mu.reshape(1, D_SHIFT), w0, w2blk, a0, a2blk, g2, k_k.reshape(1, c),
      k_a.reshape(1, c), r_k.reshape(1, c), bd)


def _scan_kernel(fs_ref, fd_ref, bs_ref, bd_ref, yf_ref, yb_ref, st_ref):
    @pl.when(pl.program_id(0) == 0)
    def _init():
        st_ref[...] = jnp.zeros_like(st_ref)

    kg = 8
    n_groups = HEAD_DIM // kg

    def one_step(d, s_ref, d_ref, y_ref, row):
        v_t = s_ref[row, 1].reshape(HEAD_DIM, -1)

        sa = jnp.zeros_like(v_t)
        for k in range(HEAD_DIM):
            sa = sa + st_ref[d, k] * s_ref[row, 2, k // kg, pl.ds(k % kg, 1), :]

        def pass_update(g, y):
            for kk in range(kg):
                k = g * kg + kk
                s_new = (st_ref[d, k] * d_ref[row, 0, g, pl.ds(kk, 1), :] - sa * d_ref[row, 2, g, pl.ds(kk, 1), :]
                         + v_t * d_ref[row, 1, g, pl.ds(kk, 1), :])
                st_ref[d, k] = s_new
                y = y + s_new * s_ref[row, 0, g, pl.ds(kk, 1), :]
            return y

        y_ref[row] = lax.fori_loop(0, n_groups, pass_update, jnp.zeros_like(v_t))

    def step(jj, carry):
        one_step(0, fs_ref, fd_ref, yf_ref, jj)
        one_step(1, bs_ref, bd_ref, yb_ref, TC_SCAN - 1 - jj)
        return carry

    lax.fori_loop(0, TC_SCAN, step, 0)


def _scan(feat_t, n_ctx):
    s, _, hd, lanes = feat_t.shape
    feat_t = feat_t.reshape(s, N_FEAT, hd // 8, 8, lanes)
    nblk = s // TC_SCAN
    nc = n_ctx // TC_SCAN
    assert n_ctx % TC_SCAN == 0 and s % TC_SCAN == 0

    def bwd(i):
        return jnp.where(i < nc, nc - 1 - i, nblk - 1 - i + nc)

    blk = (TC_SCAN, 3, hd // 8, 8, lanes)
    yblk = (TC_SCAN, hd, lanes)
    return pl.pallas_call(
        _scan_kernel,
        grid=(nblk,),
        in_specs=[pl.BlockSpec(blk, lambda i: (i, 0, 0, 0, 0)),
                  pl.BlockSpec(blk, lambda i: (i, 1, 0, 0, 0)),
                  pl.BlockSpec(blk, lambda i: (bwd(i), 0, 0, 0, 0)),
                  pl.BlockSpec(blk, lambda i: (bwd(i), 2, 0, 0, 0))],
        out_specs=[pl.BlockSpec(yblk, lambda i: (i, 0, 0)),
                   pl.BlockSpec(yblk, lambda i: (bwd(i), 0, 0))],
        out_shape=[jax.ShapeDtypeStruct((s, hd, lanes), F32)] * 2,
        scratch_shapes=[pltpu.VMEM((2, HEAD_DIM, hd, lanes), F32)],
        compiler_params=_cparams(("arbitrary",)),
        name="rwkv_scan",
    )(feat_t, feat_t, feat_t, feat_t)


def _outproj_kernel(yf_ref, yb_ref, rd_ref, yna_ref, x_ref, mod_ref, lnw_ref, lnb_ref, bd_ref, w_ref, o_ref):
    c = D_RWKV
    bd = bd_ref[...]
    y = yf_ref[0] + yb_ref[0]
    mu = _head_sum(y, bd) * (1.0 / HEAD_DIM)
    dlt = y - mu
    var = _head_sum(dlt * dlt, bd) * (1.0 / HEAD_DIM)
    yn = dlt * lax.rsqrt(var + GN_EPS) * lnw_ref[...] + lnb_ref[...]
    y_rw = ((yn + rd_ref[0, :, 0:c]) * rd_ref[0, :, c:2 * c]).astype(BF16)
    o = (jnp.dot(y_rw, w_ref[0:c, :], preferred_element_type=F32)
         + jnp.dot(yna_ref[0], w_ref[c:, :], preferred_element_type=F32))
    o_ref[0] = x_ref[0] + mod_ref[2:3, :] * o


def _outproj(yf, yb, rd, yna, xs, modtab, ln_w, ln_b, bd, w_out_bf, n_ctx_tiles):
    b, s, d = xs.shape
    nj = s // TM
    c = D_RWKV

    def tile(w):
        return pl.BlockSpec((1, TM, w), lambda b, j: (b, j, 0))

    def full(shape):
        return pl.BlockSpec(shape, lambda b, j: (0,) * len(shape))

    return pl.pallas_call(
        _outproj_kernel,
        grid=(b, nj),
        in_specs=[tile(c), tile(c), tile(2 * c), tile(D_NA), tile(d), _mod_spec(n_ctx_tiles),
                  full((1, c)), full((1, c)), full((c, c)), full((d, d))],
        out_specs=tile(d),
        out_shape=jax.ShapeDtypeStruct((b, s, d), F32),
        compiler_params=_cparams(("arbitrary", "arbitrary")),
        name="out_proj",
    )(yf, yb, rd, yna, xs, modtab, ln_w.reshape(1, c), ln_b.reshape(1, c), bd, w_out_bf)


def _ffn_kernel(x_ref, mod_ref, g_ref, w1_ref, w3_ref, w2_ref, o_ref, *, ffc):
    x = x_ref[0]
    h = _norm_mod(x, g_ref[...], mod_ref[4:5, :], mod_ref[3:4, :]).astype(BF16)
    d_ff = w1_ref.shape[1]
    acc = jnp.zeros(x.shape, F32)
    for c0 in range(0, d_ff, ffc):
        a = jnp.dot(h, w1_ref[:, c0:c0 + ffc], preferred_element_type=F32)
        g = jnp.dot(h, w3_ref[:, c0:c0 + ffc], preferred_element_type=F32)
        acc = acc + jnp.dot((_silu(a) * g).astype(BF16), w2_ref[c0:c0 + ffc, :], preferred_element_type=F32)
    o_ref[0] = x + mod_ref[5:6, :] * acc


def _ffn(xs, modtab, g, w1, w3, w2, n_ctx_tiles):
    b, s, d = xs.shape
    nj = s // TM
    d_ff = w1.shape[1]
    const = lambda b, j: (0, 0)
    return pl.pallas_call(
        functools.partial(_ffn_kernel, ffc=d_ff // 2),
        grid=(b, nj),
        in_specs=[pl.BlockSpec((1, TM, d), lambda b, j: (b, j, 0)), _mod_spec(n_ctx_tiles),
                  pl.BlockSpec((1, d), const),
                  pl.BlockSpec((d, d_ff), const), pl.BlockSpec((d, d_ff), const), pl.BlockSpec((d_ff, d), const)],
        out_specs=pl.BlockSpec((1, TM, d), lambda b, j: (b, j, 0)),
        out_shape=jax.ShapeDtypeStruct((b, s, d), F32),
        compiler_params=_cparams(("arbitrary", "arbitrary"), 56),
        name="ffn_swiglu",
    )(xs, modtab, g.reshape(1, d), w1, w3, w2)


def _router_kernel(x_ref, mod_ref, g_ref, r_ref, h_ref, slab_ref):
    h = _norm_mod(x_ref[0], g_ref[...], mod_ref[4:5, :], mod_ref[3:4, :])
    h_ref[0] = h
    logits = jnp.dot(h, r_ref[...], precision=HI, preferred_element_type=F32)
    lane = lax.broadcasted_iota(jnp.int32, logits.shape, 1)
    lg = jnp.where(lane < N_EXPERTS, logits, -jnp.inf)
    m1 = jnp.max(lg, axis=-1, keepdims=True)
    i1 = jnp.min(jnp.where(lg == m1, lane, LANES), axis=-1, keepdims=True)
    lg2 = jnp.where(lane == i1, -jnp.inf, lg)
    m2 = jnp.max(lg2, axis=-1, keepdims=True)
    i2 = jnp.min(jnp.where(lg2 == m2, lane, LANES), axis=-1, keepdims=True)
    e = jnp.exp(m2 - m1)
    g1 = 1.0 / (1.0 + e)
    g2 = e / (1.0 + e)
    slab_ref[0] = jnp.where(lane == 0, i1.astype(F32),
                            jnp.where(lane == 1, i2.astype(F32),
                                      jnp.where(lane == 2, g1, jnp.where(lane == 3, g2, 0.0))))


def _router(xs, modtab, g, router_pad, n_ctx_tiles):
    b, s, d = xs.shape
    nj = s // TM
    const = lambda b, j: (0, 0)
    return pl.pallas_call(
        _router_kernel,
        grid=(b, nj),
        in_specs=[pl.BlockSpec((1, TM, d), lambda b, j: (b, j, 0)), _mod_spec(n_ctx_tiles),
                  pl.BlockSpec((1, d), const), pl.BlockSpec((d, LANES), const)],
        out_specs=[pl.BlockSpec((1, TM, d), lambda b, j: (b, j, 0)),
                   pl.BlockSpec((1, TM, LANES), lambda b, j: (b, j, 0))],
        out_shape=[jax.ShapeDtypeStruct((b, s, d), F32), jax.ShapeDtypeStruct((b, s, LANES), F32)],
        compiler_params=_cparams(("arbitrary", "arbitrary")),
        name="moe_router",
    )(xs, modtab, g.reshape(1, d), router_pad)


def _dispatch(idx):
    n = idx.shape[0]
    a = n * 2
    flat_e = idx.reshape(a)
    onehot = (flat_e[None, :] == jnp.arange(N_EXPERTS, dtype=jnp.int32)[:, None]).astype(jnp.int32)
    csum = jnp.cumsum(onehot, axis=1)
    counts = csum[:, -1]
    rank = jnp.sum(csum * onehot, axis=0) - 1
    padded = (counts + MOE_BLOCK - 1) // MOE_BLOCK * MOE_BLOCK
    pad_ends = jnp.cumsum(padded)
    pad_starts = pad_ends - padded
    dest = (pad_starts[flat_e] + rank).astype(jnp.int32)
    n_blocks = -(-a // MOE_BLOCK) + N_EXPERTS
    slot_tok = jnp.zeros((n_blocks * MOE_BLOCK,), jnp.int32).at[dest].set(jnp.arange(a, dtype=jnp.int32) // 2)
    block_start = jnp.arange(n_blocks, dtype=jnp.int32) * MOE_BLOCK
    block_e = jnp.minimum(jnp.sum((pad_ends[None, :] <= block_start[:, None]).astype(jnp.int32), axis=1),
                          N_EXPERTS - 1)
    n_used = (pad_ends[-1] // MOE_BLOCK).astype(jnp.int32).reshape(1)
    return slot_tok, block_e, n_used, dest


def _moe_mm_kernel(be_ref, st_ref, nu_ref, h_hbm, w1_hbm, w3_hbm, w2_hbm, o_ref, xbuf, w1s, w3s, w2s, gsem, wsem,
                   *, ffc):
    i = pl.program_id(0)
    n_used = nu_ref[0]

    def row_copy(tok, slot, r):
        return pltpu.make_async_copy(h_hbm.at[pl.ds(tok, 1)], xbuf.at[slot, pl.ds(r, 1)], gsem.at[slot])

    def issue(blk, slot):
        def body(r, carry):
            row_copy(st_ref[blk * MOE_BLOCK + r], slot, r).start()
            return carry
        lax.fori_loop(0, MOE_BLOCK, body, 0)

    def wait_rows(slot):
        def body(r, carry):
            row_copy(0, slot, r).wait()
            return carry
        lax.fori_loop(0, MOE_BLOCK, body, 0)

    n_blocks = pl.num_programs(0)
    slot = lax.rem(i, 2)
    nslot = 1 - slot
    nxt = jnp.minimum(i + 1, n_blocks - 1)

    @pl.when(i == 0)
    def _first():
        issue(0, 0)

    @pl.when(i < n_used)
    def _compute():
        e = be_ref[i]
        changed = (i == 0) | (e != be_ref[jnp.maximum(i - 1, 0)])

        @pl.when(changed)
        def _load_weights():
            copies = [pltpu.make_async_copy(w1_hbm.at[e], w1s, wsem.at[0]),
                      pltpu.make_async_copy(w3_hbm.at[e], w3s, wsem.at[1]),
                      pltpu.make_async_copy(w2_hbm.at[e], w2s, wsem.at[2])]
            for cp in copies:
                cp.start()
            for cp in copies:
                cp.wait()

        wait_rows(slot)
        xb = xbuf[slot].astype(BF16)
        acc = jnp.zeros(o_ref.shape, F32)
        d_ff = w1s.shape[1]
        n_chunks = d_ff // ffc
        rows_per_chunk = -(-MOE_BLOCK // n_chunks)
        for ci in range(n_chunks):
            c0 = ci * ffc
            a = jnp.dot(xb, w1s[:, c0:c0 + ffc], preferred_element_type=F32)
            g = jnp.dot(xb, w3s[:, c0:c0 + ffc], preferred_element_type=F32)
            acc = acc + jnp.dot((_silu(a) * g).astype(BF16), w2s[c0:c0 + ffc, :], preferred_element_type=F32)
            for r in range(ci * rows_per_chunk, min((ci + 1) * rows_per_chunk, MOE_BLOCK)):
                row_copy(st_ref[nxt * MOE_BLOCK + r], nslot, r).start()
        o_ref[...] = acc

    @pl.when(i >= n_used)
    def _unused():
        o_ref[...] = jnp.zeros_like(o_ref)
        wait_rows(slot)
        issue(nxt, nslot)

    @pl.when(i == n_blocks - 1)
    def _drain():
        wait_rows(nslot)


def _moe_mm(h_flat, slot_tok, block_e, n_used, w1, w3, w2):
    n, d = h_flat.shape
    n_blocks = block_e.shape[0]
    d_ff = w1.shape[2]
    any_spec = pl.BlockSpec(memory_space=pl.ANY)
    return pl.pallas_call(
        functools.partial(_moe_mm_kernel, ffc=512),
        grid_spec=pltpu.PrefetchScalarGridSpec(
            num_scalar_prefetch=3,
            grid=(n_blocks,),
            in_specs=[any_spec, any_spec, any_spec, any_spec],
            out_specs=pl.BlockSpec((MOE_BLOCK, d), lambda i, be, st, nu: (i, 0)),
            scratch_shapes=[pltpu.VMEM((2, MOE_BLOCK, d), F32),
                            pltpu.VMEM((d, d_ff), BF16), pltpu.VMEM((d, d_ff), BF16), pltpu.VMEM((d_ff, d), BF16),
                            pltpu.SemaphoreType.DMA((2,)), pltpu.SemaphoreType.DMA((3,))]),
        out_shape=jax.ShapeDtypeStruct((n_blocks * MOE_BLOCK, d), F32),
        compiler_params=_cparams(("arbitrary",), 56),
        name="moe_grouped_swiglu",
    )(block_e, slot_tok, n_used, h_flat, w1, w3, w2)


def _moe_comb_kernel(inv_ref, x_ref, mod_ref, slab_ref, fg_ref, yb_hbm, o_ref, gbuf, gsem, *, nj, final):
    b = pl.program_id(0)
    j = pl.program_id(1)
    t = b * nj + j
    nt = pl.num_programs(0) * nj

    def row_copy(src, slot, which, r):
        return pltpu.make_async_copy(yb_hbm.at[pl.ds(src, 1)], gbuf.at[slot, which, pl.ds(r, 1)], gsem.at[slot])

    def issue(tile, slot):
        def body(r, carry):
            base = (tile * TM + r) * 2
            row_copy(inv_ref[base], slot, 0, r).start()
            row_copy(inv_ref[base + 1], slot, 1, r).start()
            return carry
        lax.fori_loop(0, TM, body, 0)

    def wait_rows(slot):
        def body(r, carry):
            row_copy(0, slot, 0, r).wait()
            row_copy(0, slot, 1, r).wait()
            return carry
        lax.fori_loop(0, TM, body, 0)

    @pl.when(t == 0)
    def _first():
        issue(0, 0)

    @pl.when(t + 1 < nt)
    def _prefetch():
        issue(t + 1, lax.rem(t + 1, 2))

    slot = lax.rem(t, 2)
    wait_rows(slot)
    slab = slab_ref[0]
    y = slab[:, 2:3] * gbuf[slot, 0] + slab[:, 3:4] * gbuf[slot, 1]
    xn = x_ref[0] + mod_ref[5:6, :] * y
    if final:
        xn = _rms(xn) * fg_ref[...]
    o_ref[0] = xn


def _moe_combine(inv, xs, modtab, slab, final_g, yb, n_ctx_tiles, final):
    b, s, d = xs.shape
    nj = s // TM
    return pl.pallas_call(
        functools.partial(_moe_comb_kernel, nj=nj, final=final),
        grid_spec=pltpu.PrefetchScalarGridSpec(
            num_scalar_prefetch=1,
            grid=(b, nj),
            in_specs=[pl.BlockSpec((1, TM, d), lambda b, j, inv: (b, j, 0)),
                      pl.BlockSpec((None, None, 6, D_MODEL),
                                   lambda b, j, inv: (b, jnp.where(j >= n_ctx_tiles, 1, 0), 0, 0)),
                      pl.BlockSpec((1, TM, LANES), lambda b, j, inv: (b, j, 0)),
                      pl.BlockSpec((1, d), lambda b, j, inv: (0, 0)),
                      pl.BlockSpec(memory_space=pl.ANY)],
            out_specs=pl.BlockSpec((1, TM, d), lambda b, j, inv: (b, j, 0)),
            scratch_shapes=[pltpu.VMEM((2, 2, TM, d), F32), pltpu.SemaphoreType.DMA((2,))]),
        out_shape=jax.ShapeDtypeStruct((b, s, d), F32),
        compiler_params=_cparams(("arbitrary", "arbitrary")),
        name="moe_combine",
    )(inv, xs, modtab, slab, final_g.reshape(1, d), yb)


def _blockdiag(w):
    z = jnp.zeros_like(w[0])
    return jnp.concatenate([jnp.concatenate([w[0], z], axis=1), jnp.concatenate([z, w[1]], axis=1)], axis=0)


def kernel(x, c, ctx, c_ctx, ada_w, ada_b, norm_mix_g, norm_ffn_g, w_in, shift_mu, w0, w2, a0, a2, g2, k_k, k_a, r_k,
           ln_x_w, ln_x_b, na_rpb, w_out, ffn_w1, ffn_w3, ffn_w2, router, moe_w1, moe_w3, moe_w2, final_g):
    b, t, d = x.shape
    n_ctx = ctx.shape[1]
    s = n_ctx + t
    depth = ada_w.shape[0]
    assert d == D_MODEL and n_ctx % TM == 0 and t % TM == 0
    n_ctx_tiles = n_ctx // TM
    lanes = b * H_RWKV

    nb = -(-(b + 1) // 8) * 8
    c_all = jnp.concatenate([c, c_ctx[None, :], jnp.zeros((nb - b - 1, d), F32)], axis=0)
    mods = _mods(c_all, ada_w, ada_b).reshape(depth, nb, 6, d)
    modtab = jnp.stack([jnp.broadcast_to(mods[:, b:b + 1], (depth, b, 6, d)), mods[:, :b]], axis=2)

    head_id = np.arange(D_RWKV) // HEAD_DIM
    bd = jnp.asarray(head_id[:, None] == head_id[None, :], BF16)

    xs = jnp.concatenate([ctx, x], axis=1)
    for l in range(depth):
        last = l == depth - 1
        qkv, u_raw = _inproj(xs, modtab[l], norm_mix_g[l], w_in[l].astype(BF16), n_ctx_tiles)
        y_na = _na(qkv, _na_bias_table(na_rpb[l]), n_ctx)
        feat, rd = _features(u_raw, n_ctx, shift_mu[l], w0[l], _blockdiag(w2[l]), a0[l], _blockdiag(a2[l]), g2[l],
                             k_k[l], k_a[l], r_k[l], bd)
        feat_t = feat.reshape(b, s, N_FEAT, H_RWKV, HEAD_DIM).transpose(1, 2, 4, 0, 3).reshape(
            s, N_FEAT, HEAD_DIM, lanes)
        yf_t, yb_t = _scan(feat_t, n_ctx)

        def untranspose(y):
            return y.reshape(s, HEAD_DIM, b, H_RWKV).transpose(2, 0, 3, 1).reshape(b, s, D_RWKV)

        xs = _outproj(untranspose(yf_t), untranspose(yb_t), rd, y_na, xs, modtab[l], ln_x_w[l], ln_x_b[l], bd,
                      w_out[l].astype(BF16), n_ctx_tiles)
        i = l // 2
        if l % 2 == 0:
            xs = _ffn(xs, modtab[l], norm_ffn_g[l], ffn_w1[i].astype(BF16), ffn_w3[i].astype(BF16),
                      ffn_w2[i].astype(BF16), n_ctx_tiles)
        else:
            router_pad = jnp.pad(router[i], ((0, 0), (0, LANES - N_EXPERTS)))
            h, slab = _router(xs, modtab[l], norm_ffn_g[l], router_pad, n_ctx_tiles)
            idx = slab[..., 0:2].astype(jnp.int32).reshape(b * s, 2)
            slot_tok, block_e, n_used, dest = _dispatch(idx)
            yb = _moe_mm(h.reshape(b * s, d), slot_tok, block_e, n_used, moe_w1[i].astype(BF16),
                         moe_w3[i].astype(BF16), moe_w2[i].astype(BF16))
            xs = _moe_combine(dest, xs, modtab[l], slab, final_g, yb, n_ctx_tiles, final=last)
    if depth % 2 == 1:
        raise NotImplementedError("final norm is fused into the MoE combine of the last (odd) layer")
    return xs[:, n_ctx:, :]
```

```python
import functools

import numpy as np
import jax
import jax.numpy as jnp
from jax import lax
from jax.experimental import pallas as pl
from jax.experimental.pallas import tpu as pltpu

F32 = jnp.float32
BF16 = jnp.bfloat16
HI = lax.Precision.HIGHEST

D_MODEL = 1024
HEAD_DIM = 64
GRID_W = 64
D_RWKV = 512
D_NA = 512
H_RWKV = D_RWKV // HEAD_DIM
H_NA = D_NA // HEAD_DIM
R_DECAY = 64
R_AAA = 64
R_GATE = 128
D_SHIFT = 3 * D_RWKV + 2 * R_DECAY + 2 * R_AAA + R_GATE
N_QKV = 3 * D_NA
N_EXPERTS = 8
MOE_BLOCK = 256
RMS_EPS = 1e-6
GN_EPS = 64e-5
L2_EPS = 1e-12
NEG = -1e30

TM = 256
NA_WIN_ROWS = 12
N_FEAT = 9
TC_SCAN = 16
LANES = 128


def _cparams(sem, vmem_mb=48):
    return pltpu.CompilerParams(dimension_semantics=sem, vmem_limit_bytes=vmem_mb * 1024 * 1024)


def _sigmoid(x):
    return 1.0 / (1.0 + jnp.exp(-x))


def _silu(x):
    return x * _sigmoid(x)


def _softplus(x):
    return jnp.maximum(x, 0.0) + jnp.log(1.0 + jnp.exp(-jnp.abs(x)))


def _rms(x):
    return x * lax.rsqrt(jnp.mean(x * x, axis=-1, keepdims=True) + RMS_EPS)


def _norm_mod(x, g, scale, shift):
    return _rms(x) * g * (1.0 + scale) + shift


def _head_sum(x, bd):
    hi = x.astype(BF16)
    lo = (x - hi.astype(F32)).astype(BF16)
    return jnp.dot(hi, bd, preferred_element_type=F32) + jnp.dot(lo, bd, preferred_element_type=F32)


def _dot_nt(a, b):
    return lax.dot_general(a, b, (((1,), (1,)), ((), ())), preferred_element_type=F32)


def _mods_kernel(c_ref, w_ref, b_ref, o_ref):
    o_ref[0] = jnp.dot(_silu(c_ref[...]), w_ref[0], precision=HI, preferred_element_type=F32) + b_ref[0]


def _mods(c_all, ada_w, ada_b):
    depth, d, n6 = ada_w.shape
    nb = c_all.shape[0]
    tn = 1536
    return pl.pallas_call(
        _mods_kernel,
        grid=(depth, n6 // tn),
        in_specs=[pl.BlockSpec((nb, d), lambda l, n: (0, 0)),
                  pl.BlockSpec((1, d, tn), lambda l, n: (l, 0, n)),
                  pl.BlockSpec((1, 1, tn), lambda l, n: (l, 0, n))],
        out_specs=pl.BlockSpec((1, nb, tn), lambda l, n: (l, 0, n)),
        out_shape=jax.ShapeDtypeStruct((depth, nb, n6), F32),
        compiler_params=_cparams(("arbitrary", "arbitrary")),
        name="adaln_mods",
    )(c_all, ada_w, ada_b.reshape(depth, 1, n6))


def _mod_spec(n_ctx_tiles):
    return pl.BlockSpec((None, None, 6, D_MODEL), lambda b, j: (b, jnp.where(j >= n_ctx_tiles, 1, 0), 0, 0))


def _inproj_kernel(x_ref, mod_ref, g_ref, w_ref, qkv_ref, u_ref):
    h = _norm_mod(x_ref[0], g_ref[...], mod_ref[1:2, :], mod_ref[0:1, :]).astype(BF16)
    qkv_ref[0] = jnp.dot(h, w_ref[:, :N_QKV], preferred_element_type=F32).astype(BF16)
    u_ref[0] = jnp.dot(h, w_ref[:, N_QKV:], preferred_element_type=F32)


def _inproj(xs, modtab, g, w_in_bf, n_ctx_tiles):
    b, s, d = xs.shape
    nj = s // TM
    return pl.pallas_call(
        _inproj_kernel,
        grid=(b, nj),
        in_specs=[pl.BlockSpec((1, TM, d), lambda b, j: (b, j, 0)),
                  _mod_spec(n_ctx_tiles),
                  pl.BlockSpec((1, d), lambda b, j: (0, 0)),
                  pl.BlockSpec((d, N_QKV + D_SHIFT), lambda b, j: (0, 0))],
        out_specs=[pl.BlockSpec((1, TM, N_QKV), lambda b, j: (b, j, 0)),
                   pl.BlockSpec((1, TM, D_SHIFT), lambda b, j: (b, j, 0))],
        out_shape=[jax.ShapeDtypeStruct((b, s, N_QKV), BF16),
                   jax.ShapeDtypeStruct((b, s, D_SHIFT), F32)],
        compiler_params=_cparams(("arbitrary", "arbitrary")),
        name="in_proj",
    )(xs, modtab, g.reshape(1, d), w_in_bf)


def _na_bias_table(rpb):
    nh = rpb.shape[0]
    cq = np.arange(GRID_W)[:, None]
    ck = np.arange(GRID_W)[None, :]
    c0 = np.clip(cq - 8, 0, GRID_W - 16)
    col_ok = (ck >= c0) & (ck < c0 + 16)
    col_off = np.clip(ck - cq + 15, 0, 30)
    t = jnp.take(rpb, jnp.asarray(col_off.reshape(-1), jnp.int32), axis=2).reshape(nh, 15, GRID_W, GRID_W)
    t = jnp.where(col_ok[None, None], t, NEG)
    tabs = []
    for case in range(3):
        per_row = []
        for i in range(4):
            lo, ro = ((0, 7 - i), (i, 3), (4, 3 - i))[case]
            blk = jnp.pad(t[:, ro:ro + 8], ((0, 0), (lo, NA_WIN_ROWS - 8 - lo), (0, 0), (0, 0)),
                          constant_values=NEG)
            per_row.append(blk.transpose(0, 2, 1, 3))
        tabs.append(jnp.stack(per_row, axis=1).reshape(nh, TM, NA_WIN_ROWS * GRID_W))
    return jnp.stack(tabs, axis=1)


def _na_kernel(q_ref, k_ref, v_ref, bias_ref, o_ref, *, n_ctx, rows):
    j = pl.program_id(1)
    n_ctx_tiles = n_ctx // TM
    scale = HEAD_DIM ** -0.5
    nwin = NA_WIN_ROWS * GRID_W

    def head(h):
        return slice(h * HEAD_DIM, (h + 1) * HEAD_DIM)

    @pl.when(j < n_ctx_tiles)
    def _ctx():
        for h in range(H_NA):
            q = q_ref[0, :, head(h)]
            s = _dot_nt(q, k_ref[0, 0:n_ctx, head(h)]) * scale
            p = jnp.exp(s - jnp.max(s, axis=-1, keepdims=True))
            y = jnp.dot(p.astype(BF16), v_ref[0, 0:n_ctx, head(h)], preferred_element_type=F32)
            o_ref[0, :, head(h)] = (y / jnp.sum(p, axis=-1, keepdims=True)).astype(BF16)

    @pl.when(j >= n_ctx_tiles)
    def _lat():
        m = j - n_ctx_tiles
        w0 = jnp.clip(4 * m - 4, 0, rows - NA_WIN_ROWS)
        start = pl.multiple_of(n_ctx + w0 * GRID_W, GRID_W)
        for h in range(H_NA):
            q = q_ref[0, :, head(h)]
            s_loc = _dot_nt(q, k_ref[0, pl.ds(start, nwin), head(h)]) * scale + bias_ref[h]
            s_ctx = _dot_nt(q, k_ref[0, 0:n_ctx, head(h)]) * scale
            mx = jnp.maximum(jnp.max(s_loc, axis=-1, keepdims=True), jnp.max(s_ctx, axis=-1, keepdims=True))
            p_loc = jnp.exp(s_loc - mx)
            p_ctx = jnp.exp(s_ctx - mx)
            den = jnp.sum(p_loc, axis=-1, keepdims=True) + jnp.sum(p_ctx, axis=-1, keepdims=True)
            y = (jnp.dot(p_loc.astype(BF16), v_ref[0, pl.ds(start, nwin), head(h)], preferred_element_type=F32)
                 + jnp.dot(p_ctx.astype(BF16), v_ref[0, 0:n_ctx, head(h)], preferred_element_type=F32))
            o_ref[0, :, head(h)] = (y / den).astype(BF16)


def _na(qkv, bias_tab, n_ctx):
    b, s, _ = qkv.shape
    nj = s // TM
    n_ctx_tiles = n_ctx // TM
    rows = (s - n_ctx) // GRID_W
    n_lat_tiles = nj - n_ctx_tiles
    assert rows >= NA_WIN_ROWS and rows % 4 == 0 and n_lat_tiles >= 3

    def case_map(b, j):
        m = j - n_ctx_tiles
        return (0, jnp.where(m <= 0, 0, jnp.where(m >= n_lat_tiles - 1, 2, 1)), 0, 0)

    return pl.pallas_call(
        functools.partial(_na_kernel, n_ctx=n_ctx, rows=rows),
        grid=(b, nj),
        in_specs=[pl.BlockSpec((1, TM, D_NA), lambda b, j: (b, j, 0)),
                  pl.BlockSpec((1, s, D_NA), lambda b, j: (b, 0, 1)),
                  pl.BlockSpec((1, s, D_NA), lambda b, j: (b, 0, 2)),
                  pl.BlockSpec((H_NA, None, TM, NA_WIN_ROWS * GRID_W), case_map)],
        out_specs=pl.BlockSpec((1, TM, D_NA), lambda b, j: (b, j, 0)),
        out_shape=jax.ShapeDtypeStruct((b, s, D_NA), BF16),
        compiler_params=_cparams(("arbitrary", "arbitrary")),
        name="na_attention",
    )(qkv, qkv, qkv, bias_tab)


def _feat_kernel(u_ref, up_ref, un_ref, mu_ref, w0_ref, w2_ref, a0_ref, a2_ref, g2_ref, kk_ref, ka_ref, rk_ref,
                 bd_ref, feat_ref, rd_ref, *, n_ctx_tiles, nj):
    j = pl.program_id(1)
    seg_start = (j == 0) | (j == n_ctx_tiles)
    seg_end = (j == n_ctx_tiles - 1) | (j == nj - 1)
    rid = lax.broadcasted_iota(jnp.int32, (TM, 1), 0)

    def shifted(lo, hi):
        p = u_ref[0, :, lo:hi]
        prev_row = jnp.where(seg_start, 0.0, up_ref[0, 7:8, lo:hi])
        next_row = jnp.where(seg_end, 0.0, un_ref[0, 0:1, lo:hi])
        prev = jnp.where(rid == 0, prev_row, pltpu.roll(p, 1, axis=0))
        nxt = jnp.where(rid == TM - 1, next_row, pltpu.roll(p, TM - 1, axis=0))
        return p + mu_ref[:, lo:hi] * (0.5 * (prev + nxt) - p)

    c = D_RWKV
    r = shifted(0, c)
    k = shifted(c, 2 * c)
    v = shifted(2 * c, 3 * c)
    o = 3 * c
    wl = shifted(o, o + 2 * R_DECAY)
    al = shifted(o + 2 * R_DECAY, o + 2 * R_DECAY + 2 * R_AAA)
    gl = shifted(o + 2 * R_DECAY + 2 * R_AAA, D_SHIFT)
    bd = bd_ref[...]

    lw = jnp.dot(jnp.tanh(wl), w2_ref[...], precision=HI, preferred_element_type=F32)
    la = jnp.dot(al, a2_ref[...], precision=HI, preferred_element_type=F32)
    g = jnp.dot(_sigmoid(gl), g2_ref[...], precision=HI, preferred_element_type=F32)

    kks = k * kk_ref[...]
    kk = kks / jnp.maximum(jnp.sqrt(_head_sum(kks * kks, bd)), L2_EPS)
    feat_ref[0, :, 0:c] = r
    feat_ref[0, :, c:2 * c] = v
    feat_ref[0, :, 2 * c:3 * c] = kk
    ksum = jnp.zeros_like(k)
    for z in range(2):
        w_log = -_softplus(-(w0_ref[z:z + 1, :] + lw[:, z * c:(z + 1) * c])) - 0.5
        decay = jnp.exp(-jnp.exp(w_log))
        a = _sigmoid(a0_ref[z:z + 1, :] + la[:, z * c:(z + 1) * c])
        k_dir = k * (1.0 + (a - 1.0) * ka_ref[...])
        ksum = ksum + k_dir
        base = (3 + 3 * z) * c
        feat_ref[0, :, base:base + c] = decay
        feat_ref[0, :, base + c:base + 2 * c] = k_dir
        feat_ref[0, :, base + 2 * c:base + 3 * c] = kk * a
    bonus = _head_sum(r * (0.5 * ksum) * rk_ref[...], bd) * v
    rd_ref[0, :, 0:c] = bonus
    rd_ref[0, :, c:2 * c] = g


def _features(u_raw, n_ctx, shift_mu, w0, w2blk, a0, a2blk, g2, k_k, k_a, r_k, bd):
    b, s, _ = u_raw.shape
    nj = s // TM
    n_ctx_tiles = n_ctx // TM
    c = D_RWKV
    t8 = TM // 8

    def full(shape):
        return pl.BlockSpec(shape, lambda b, j: (0,) * len(shape))

    return pl.pallas_call(
        functools.partial(_feat_kernel, n_ctx_tiles=n_ctx_tiles, nj=nj),
        grid=(b, nj),
        in_specs=[pl.BlockSpec((1, TM, D_SHIFT), lambda b, j: (b, j, 0)),
                  pl.BlockSpec((1, 8, D_SHIFT), lambda b, j: (b, jnp.maximum(j * t8 - 1, 0), 0)),
                  pl.BlockSpec((1, 8, D_SHIFT), lambda b, j: (b, jnp.minimum((j + 1) * t8, s // 8 - 1), 0)),
                  full((1, D_SHIFT)), full((2, c)), full((2 * R_DECAY, 2 * c)), full((2, c)),
                  full((2 * R_AAA, 2 * c)), full((R_GATE, c)), full((1, c)), full((1, c)), full((1, c)),
                  full((c, c))],
        out_specs=[pl.BlockSpec((1, TM, N_FEAT * c), lambda b, j: (b, j, 0)),
                   pl.BlockSpec((1, TM, 2 * c), lambda b, j: (b, j, 0))],
        out_shape=[jax.ShapeDtypeStruct((b, s, N_FEAT * c), F32),
                   jax.ShapeDtypeStruct((b, s, 2 * c), F32)],
        compiler_params=_cparams(("arbitrary", "arbitrary")),
        name="rwkv_features",
    )(u_raw, u_raw, u_raw, shift_mu.reshape(1, D_SHIFT), w0, w2blk, a0, a2blk, g2, k_k.reshape(1, c),
      k_a.reshape(1, c), r_k.reshape(1, c), bd)


def _scan_kernel(fs_ref, fd_ref, bs_ref, bd_ref, yf_ref, yb_ref, st_ref):
    @pl.when(pl.program_id(0) == 0)
    def _init():
        st_ref[...] = jnp.zeros_like(st_ref)

    kg = 8
    n_groups = HEAD_DIM // kg

    def one_step(d, s_ref, d_ref, y_ref, row):
        v_t = s_ref[row, 1].reshape(HEAD_DIM, -1)

        sa = jnp.zeros_like(v_t)
        for k in range(HEAD_DIM):
            sa = sa + st_ref[d, k] * s_ref[row, 2, k // kg, pl.ds(k % kg, 1), :]

        def pass_update(g, y):
            for kk in range(kg):
                k = g * kg + kk
                s_new = (st_ref[d, k] * d_ref[row, 0, g, pl.ds(kk, 1), :] - sa * d_ref[row, 2, g, pl.ds(kk, 1), :]
                         + v_t * d_ref[row, 1, g, pl.ds(kk, 1), :])
                st_ref[d, k] = s_new
                y = y + s_new * s_ref[row, 0, g, pl.ds(kk, 1), :]
            return y

        y_ref[row] = lax.fori_loop(0, n_groups, pass_update, jnp.zeros_like(v_t))

    def step(jj, carry):
        one_step(0, fs_ref, fd_ref, yf_ref, jj)
        one_step(1, bs_ref, bd_ref, yb_ref, TC_SCAN - 1 - jj)
        return carry

    lax.fori_loop(0, TC_SCAN, step, 0)


def _scan(feat_t, n_ctx):
    s, _, hd, lanes = feat_t.shape
    feat_t = feat_t.reshape(s, N_FEAT, hd // 8, 8, lanes)
    nblk = s // TC_SCAN
    nc = n_ctx // TC_SCAN
    assert n_ctx % TC_SCAN == 0 and s % TC_SCAN == 0

    def bwd(i):
        return jnp.where(i < nc, nc - 1 - i, nblk - 1 - i + nc)

    blk = (TC_SCAN, 3, hd // 8, 8, lanes)
    yblk = (TC_SCAN, hd, lanes)
    return pl.pallas_call(
        _scan_kernel,
        grid=(nblk,),
        in_specs=[pl.BlockSpec(blk, lambda i: (i, 0, 0, 0, 0)),
                  pl.BlockSpec(blk, lambda i: (i, 1, 0, 0, 0)),
                  pl.BlockSpec(blk, lambda i: (bwd(i), 0, 0, 0, 0)),
                  pl.BlockSpec(blk, lambda i: (bwd(i), 2, 0, 0, 0))],
        out_specs=[pl.BlockSpec(yblk, lambda i: (i, 0, 0)),
                   pl.BlockSpec(yblk, lambda i: (bwd(i), 0, 0))],
        out_shape=[jax.ShapeDtypeStruct((s, hd, lanes), F32)] * 2,
        scratch_shapes=[pltpu.VMEM((2, HEAD_DIM, hd, lanes), F32)],
        compiler_params=_cparams(("arbitrary",)),
        name="rwkv_scan",
    )(feat_t, feat_t, feat_t, feat_t)


def _outproj_kernel(yf_ref, yb_ref, rd_ref, yna_ref, x_ref, mod_ref, lnw_ref, lnb_ref, bd_ref, w_ref, o_ref):
    c = D_RWKV
    bd = bd_ref[...]
    y = yf_ref[0] + yb_ref[0]
    mu = _head_sum(y, bd) * (1.0 / HEAD_DIM)
    dlt = y - mu
    var = _head_sum(dlt * dlt, bd) * (1.0 / HEAD_DIM)
    yn = dlt * lax.rsqrt(var + GN_EPS) * lnw_ref[...] + lnb_ref[...]
    y_rw = ((yn + rd_ref[0, :, 0:c]) * rd_ref[0, :, c:2 * c]).astype(BF16)
    o = (jnp.dot(y_rw, w_ref[0:c, :], preferred_element_type=F32)
         + jnp.dot(yna_ref[0], w_ref[c:, :], preferred_element_type=F32))
    o_ref[0] = x_ref[0] + mod_ref[2:3, :] * o


def _outproj(yf, yb, rd, yna, xs, modtab, ln_w, ln_b, bd, w_out_bf, n_ctx_tiles):
    b, s, d = xs.shape
    nj = s // TM
    c = D_RWKV

    def tile(w):
        return pl.BlockSpec((1, TM, w), lambda b, j: (b, j, 0))

    def full(shape):
        return pl.BlockSpec(shape, lambda b, j: (0,) * len(shape))

    return pl.pallas_call(
        _outproj_kernel,
        grid=(b, nj),
        in_specs=[tile(c), tile(c), tile(2 * c), tile(D_NA), tile(d), _mod_spec(n_ctx_tiles),
                  full((1, c)), full((1, c)), full((c, c)), full((d, d))],
        out_specs=tile(d),
        out_shape=jax.ShapeDtypeStruct((b, s, d), F32),
        compiler_params=_cparams(("arbitrary", "arbitrary")),
        name="out_proj",
    )(yf, yb, rd, yna, xs, modtab, ln_w.reshape(1, c), ln_b.reshape(1, c), bd, w_out_bf)


def _ffn_kernel(x_ref, mod_ref, g_ref, w1_ref, w3_ref, w2_ref, o_ref, *, ffc):
    x = x_ref[0]
    h = _norm_mod(x, g_ref[...], mod_ref[4:5, :], mod_ref[3:4, :]).astype(BF16)
    d_ff = w1_ref.shape[1]
    acc = jnp.zeros(x.shape, F32)
    for c0 in range(0, d_ff, ffc):
        a = jnp.dot(h, w1_ref[:, c0:c0 + ffc], preferred_element_type=F32)
        g = jnp.dot(h, w3_ref[:, c0:c0 + ffc], preferred_element_type=F32)
        acc = acc + jnp.dot((_silu(a) * g).astype(BF16), w2_ref[c0:c0 + ffc, :], preferred_element_type=F32)
    o_ref[0] = x + mod_ref[5:6, :] * acc


def _ffn(xs, modtab, g, w1, w3, w2, n_ctx_tiles):
    b, s, d = xs.shape
    nj = s // TM
    d_ff = w1.shape[1]
    const = lambda b, j: (0, 0)
    return pl.pallas_call(
        functools.partial(_ffn_kernel, ffc=d_ff // 2),
        grid=(b, nj),
        in_specs=[pl.BlockSpec((1, TM, d), lambda b, j: (b, j, 0)), _mod_spec(n_ctx_tiles),
                  pl.BlockSpec((1, d), const),
                  pl.BlockSpec((d, d_ff), const), pl.BlockSpec((d, d_ff), const), pl.BlockSpec((d_ff, d), const)],
        out_specs=pl.BlockSpec((1, TM, d), lambda b, j: (b, j, 0)),
        out_shape=jax.ShapeDtypeStruct((b, s, d), F32),
        compiler_params=_cparams(("arbitrary", "arbitrary"), 56),
        name="ffn_swiglu",
    )(xs, modtab, g.reshape(1, d), w1, w3, w2)


def _router_kernel(x_ref, mod_ref, g_ref, r_ref, h_ref, slab_ref):
    h = _norm_mod(x_ref[0], g_ref[...], mod_ref[4:5, :], mod_ref[3:4, :])
    h_ref[0] = h
    logits = jnp.dot(h, r_ref[...], precision=HI, preferred_element_type=F32)
    lane = lax.broadcasted_iota(jnp.int32, logits.shape, 1)
    lg = jnp.where(lane < N_EXPERTS, logits, -jnp.inf)
    m1 = jnp.max(lg, axis=-1, keepdims=True)
    i1 = jnp.min(jnp.where(lg == m1, lane, LANES), axis=-1, keepdims=True)
    lg2 = jnp.where(lane == i1, -jnp.inf, lg)
    m2 = jnp.max(lg2, axis=-1, keepdims=True)
    i2 = jnp.min(jnp.where(lg2 == m2, lane, LANES), axis=-1, keepdims=True)
    e = jnp.exp(m2 - m1)
    g1 = 1.0 / (1.0 + e)
    g2 = e / (1.0 + e)
    slab_ref[0] = jnp.where(lane == 0, i1.astype(F32),
                            jnp.where(lane == 1, i2.astype(F32),
                                      jnp.where(lane == 2, g1, jnp.where(lane == 3, g2, 0.0))))


def _router(xs, modtab, g, router_pad, n_ctx_tiles):
    b, s, d = xs.shape
    nj = s // TM
    const = lambda b, j: (0, 0)
    return pl.pallas_call(
        _router_kernel,
        grid=(b, nj),
        in_specs=[pl.BlockSpec((1, TM, d), lambda b, j: (b, j, 0)), _mod_spec(n_ctx_tiles),
                  pl.BlockSpec((1, d), const), pl.BlockSpec((d, LANES), const)],
        out_specs=[pl.BlockSpec((1, TM, d), lambda b, j: (b, j, 0)),
                   pl.BlockSpec((1, TM, LANES), lambda b, j: (b, j, 0))],
        out_shape=[jax.ShapeDtypeStruct((b, s, d), F32), jax.ShapeDtypeStruct((b, s, LANES), F32)],
        compiler_params=_cparams(("arbitrary", "arbitrary")),
        name="moe_router",
    )(xs, modtab, g.reshape(1, d), router_pad)


def _dispatch(idx):
    n = idx.shape[0]
    a = n * 2
    flat_e = idx.reshape(a)
    onehot = (flat_e[None, :] == jnp.arange(N_EXPERTS, dtype=jnp.int32)[:, None]).astype(jnp.int32)
    csum = jnp.cumsum(onehot, axis=1)
    counts = csum[:, -1]
    rank = jnp.sum(csum * onehot, axis=0) - 1
    padded = (counts + MOE_BLOCK - 1) // MOE_BLOCK * MOE_BLOCK
    pad_ends = jnp.cumsum(padded)
    pad_starts = pad_ends - padded
    dest = (pad_starts[flat_e] + rank).astype(jnp.int32)
    n_blocks = -(-a // MOE_BLOCK) + N_EXPERTS
    slot_tok = jnp.zeros((n_blocks * MOE_BLOCK,), jnp.int32).at[dest].set(jnp.arange(a, dtype=jnp.int32) // 2)
    block_start = jnp.arange(n_blocks, dtype=jnp.int32) * MOE_BLOCK
    block_e = jnp.minimum(jnp.sum((pad_ends[None, :] <= block_start[:, None]).astype(jnp.int32), axis=1),
                          N_EXPERTS - 1)
    n_used = (pad_ends[-1] // MOE_BLOCK).astype(jnp.int32).reshape(1)
    return slot_tok, block_e, n_used, dest


def _moe_mm_kernel(be_ref, st_ref, nu_ref, h_hbm, w1_hbm, w3_hbm, w2_hbm, o_ref, xbuf, w1s, w3s, w2s, gsem, wsem,
                   *, ffc):
    i = pl.program_id(0)
    n_used = nu_ref[0]

    def row_copy(tok, slot, r):
        return pltpu.make_async_copy(h_hbm.at[pl.ds(tok, 1)], xbuf.at[slot, pl.ds(r, 1)], gsem.at[slot])

    def issue(blk, slot):
        def body(r, carry):
            row_copy(st_ref[blk * MOE_BLOCK + r], slot, r).start()
            return carry
        lax.fori_loop(0, MOE_BLOCK, body, 0, unroll=8)

    def wait_rows(slot):
        for r in range(MOE_BLOCK):
            row_copy(0, slot, r).wait()

    n_blocks = pl.num_programs(0)
    slot = lax.rem(i, 2)
    nslot = 1 - slot
    nxt = jnp.minimum(i + 1, n_blocks - 1)

    @pl.when(i == 0)
    def _first():
        issue(0, 0)

    @pl.when(i < n_used)
    def _compute():
        e = be_ref[i]
        changed = (i == 0) | (e != be_ref[jnp.maximum(i - 1, 0)])

        @pl.when(changed)
        def _load_weights():
            copies = [pltpu.make_async_copy(w1_hbm.at[e], w1s, wsem.at[0]),
                      pltpu.make_async_copy(w3_hbm.at[e], w3s, wsem.at[1]),
                      pltpu.make_async_copy(w2_hbm.at[e], w2s, wsem.at[2])]
            for cp in copies:
                cp.start()
            for cp in copies:
                cp.wait()

        wait_rows(slot)
        xb = xbuf[slot].astype(BF16)
        acc = jnp.zeros(o_ref.shape, F32)
        d_ff = w1s.shape[1]
        n_chunks = d_ff // ffc
        rows_per_chunk = -(-MOE_BLOCK // n_chunks)
        for ci in range(n_chunks):
            c0 = ci * ffc
            a = jnp.dot(xb, w1s[:, c0:c0 + ffc], preferred_element_type=F32)
            g = jnp.dot(xb, w3s[:, c0:c0 + ffc], preferred_element_type=F32)
            acc = acc + jnp.dot((_silu(a) * g).astype(BF16), w2s[c0:c0 + ffc, :], preferred_element_type=F32)
            for r in range(ci * rows_per_chunk, min((ci + 1) * rows_per_chunk, MOE_BLOCK)):
                row_copy(st_ref[nxt * MOE_BLOCK + r], nslot, r).start()
        o_ref[...] = acc

    @pl.when(i >= n_used)
    def _unused():
        o_ref[...] = jnp.zeros_like(o_ref)
        wait_rows(slot)
        issue(nxt, nslot)

    @pl.when(i == n_blocks - 1)
    def _drain():
        wait_rows(nslot)


def _moe_mm(h_flat, slot_tok, block_e, n_used, w1, w3, w2):
    n, d = h_flat.shape
    n_blocks = block_e.shape[0]
    d_ff = w1.shape[2]
    any_spec = pl.BlockSpec(memory_space=pl.ANY)
    return pl.pallas_call(
        functools.partial(_moe_mm_kernel, ffc=512),
        grid_spec=pltpu.PrefetchScalarGridSpec(
            num_scalar_prefetch=3,
            grid=(n_blocks,),
            in_specs=[any_spec, any_spec, any_spec, any_spec],
            out_specs=pl.BlockSpec((MOE_BLOCK, d), lambda i, be, st, nu: (i, 0)),
            scratch_shapes=[pltpu.VMEM((2, MOE_BLOCK, d), F32),
                            pltpu.VMEM((d, d_ff), BF16), pltpu.VMEM((d, d_ff), BF16), pltpu.VMEM((d_ff, d), BF16),
                            pltpu.SemaphoreType.DMA((2,)), pltpu.SemaphoreType.DMA((3,))]),
        out_shape=jax.ShapeDtypeStruct((n_blocks * MOE_BLOCK, d), F32),
        compiler_params=_cparams(("arbitrary",), 56),
        name="moe_grouped_swiglu",
    )(block_e, slot_tok, n_used, h_flat, w1, w3, w2)


def _moe_comb_kernel(inv_ref, x_ref, mod_ref, slab_ref, fg_ref, yb_hbm, o_ref, gbuf, gsem, *, nj, final):
    b = pl.program_id(0)
    j = pl.program_id(1)
    t = b * nj + j
    nt = pl.num_programs(0) * nj

    def row_copy(src, slot, which, r):
        return pltpu.make_async_copy(yb_hbm.at[pl.ds(src, 1)], gbuf.at[slot, which, pl.ds(r, 1)], gsem.at[slot])

    def issue(tile, slot):
        def body(r, carry):
            base = (tile * TM + r) * 2
            row_copy(inv_ref[base], slot, 0, r).start()
            row_copy(inv_ref[base + 1], slot, 1, r).start()
            return carry
        lax.fori_loop(0, TM, body, 0, unroll=8)

    def wait_rows(slot):
        for r in range(TM):
            row_copy(0, slot, 0, r).wait()
            row_copy(0, slot, 1, r).wait()

    @pl.when(t == 0)
    def _first():
        issue(0, 0)

    @pl.when(t + 1 < nt)
    def _prefetch():
        issue(t + 1, lax.rem(t + 1, 2))

    slot = lax.rem(t, 2)
    wait_rows(slot)
    slab = slab_ref[0]
    y = slab[:, 2:3] * gbuf[slot, 0] + slab[:, 3:4] * gbuf[slot, 1]
    xn = x_ref[0] + mod_ref[5:6, :] * y
    if final:
        xn = _rms(xn) * fg_ref[...]
    o_ref[0] = xn


def _moe_combine(inv, xs, modtab, slab, final_g, yb, n_ctx_tiles, final):
    b, s, d = xs.shape
    nj = s // TM
    return pl.pallas_call(
        functools.partial(_moe_comb_kernel, nj=nj, final=final),
        grid_spec=pltpu.PrefetchScalarGridSpec(
            num_scalar_prefetch=1,
            grid=(b, nj),
            in_specs=[pl.BlockSpec((1, TM, d), lambda b, j, inv: (b, j, 0)),
                      pl.BlockSpec((None, None, 6, D_MODEL),
                                   lambda b, j, inv: (b, jnp.where(j >= n_ctx_tiles, 1, 0), 0, 0)),
                      pl.BlockSpec((1, TM, LANES), lambda b, j, inv: (b, j, 0)),
                      pl.BlockSpec((1, d), lambda b, j, inv: (0, 0)),
                      pl.BlockSpec(memory_space=pl.ANY)],
            out_specs=pl.BlockSpec((1, TM, d), lambda b, j, inv: (b, j, 0)),
            scratch_shapes=[pltpu.VMEM((2, 2, TM, d), F32), pltpu.SemaphoreType.DMA((2,))]),
        out_shape=jax.ShapeDtypeStruct((b, s, d), F32),
        compiler_params=_cparams(("arbitrary", "arbitrary")),
        name="moe_combine",
    )(inv, xs, modtab, slab, final_g.reshape(1, d), yb)


def _blockdiag(w):
    z = jnp.zeros_like(w[0])
    return jnp.concatenate([jnp.concatenate([w[0], z], axis=1), jnp.concatenate([z, w[1]], axis=1)], axis=0)


def kernel(x, c, ctx, c_ctx, ada_w, ada_b, norm_mix_g, norm_ffn_g, w_in, shift_mu, w0, w2, a0, a2, g2, k_k, k_a, r_k,
           ln_x_w, ln_x_b, na_rpb, w_out, ffn_w1, ffn_w3, ffn_w2, router, moe_w1, moe_w3, moe_w2, final_g):
    b, t, d = x.shape
    n_ctx = ctx.shape[1]
    s = n_ctx + t
    depth = ada_w.shape[0]
    assert d == D_MODEL and n_ctx % TM == 0 and t % TM == 0
    n_ctx_tiles = n_ctx // TM
    lanes = b * H_RWKV

    nb = -(-(b + 1) // 8) * 8
    c_all = jnp.concatenate([c, c_ctx[None, :], jnp.zeros((nb - b - 1, d), F32)], axis=0)
    mods = _mods(c_all, ada_w, ada_b).reshape(depth, nb, 6, d)
    modtab = jnp.stack([jnp.broadcast_to(mods[:, b:b + 1], (depth, b, 6, d)), mods[:, :b]], axis=2)

    head_id = np.arange(D_RWKV) // HEAD_DIM
    bd = jnp.asarray(head_id[:, None] == head_id[None, :], BF16)

    xs = jnp.concatenate([ctx, x], axis=1)
    for l in range(depth):
        last = l == depth - 1
        qkv, u_raw = _inproj(xs, modtab[l], norm_mix_g[l], w_in[l].astype(BF16), n_ctx_tiles)
        y_na = _na(qkv, _na_bias_table(na_rpb[l]), n_ctx)
        feat, rd = _features(u_raw, n_ctx, shift_mu[l], w0[l], _blockdiag(w2[l]), a0[l], _blockdiag(a2[l]), g2[l],
                             k_k[l], k_a[l], r_k[l], bd)
        feat_t = feat.reshape(b, s, N_FEAT, H_RWKV, HEAD_DIM).transpose(1, 2, 4, 0, 3).reshape(
            s, N_FEAT, HEAD_DIM, lanes)
        yf_t, yb_t = _scan(feat_t, n_ctx)

        def untranspose(y):
            return y.reshape(s, HEAD_DIM, b, H_RWKV).transpose(2, 0, 3, 1).reshape(b, s, D_RWKV)

        xs = _outproj(untranspose(yf_t), untranspose(yb_t), rd, y_na, xs, modtab[l], ln_x_w[l], ln_x_b[l], bd,
                      w_out[l].astype(BF16), n_ctx_tiles)
        i = l // 2
        if l % 2 == 0:
            xs = _ffn(xs, modtab[l], norm_ffn_g[l], ffn_w1[i].astype(BF16), ffn_w3[i].astype(BF16),
                      ffn_w2[i].astype(BF16), n_ctx_tiles)
        else:
            xm, nct = (xs[:, n_ctx:], 0) if last else (xs, n_ctx_tiles)
            n_tok = xm.shape[0] * xm.shape[1]
            router_pad = jnp.pad(router[i], ((0, 0), (0, LANES - N_EXPERTS)))
            h, slab = _router(xm, modtab[l], norm_ffn_g[l], router_pad, nct)
            idx = slab[..., 0:2].astype(jnp.int32).reshape(n_tok, 2)
            slot_tok, block_e, n_used, dest = _dispatch(idx)
            yb = _moe_mm(h.reshape(n_tok, d), slot_tok, block_e, n_used, moe_w1[i].astype(BF16),
                         moe_w3[i].astype(BF16), moe_w2[i].astype(BF16))
            xs = _moe_combine(dest, xm, modtab[l], slab, final_g, yb, nct, final=last)
    if depth % 2 == 1:
        raise NotImplementedError("final norm is fused into the MoE combine of the last (odd) layer")
    return xs
```

```python
import functools

import numpy as np
import jax
import jax.numpy as jnp
from jax import lax
from jax.experimental import pallas as pl
from jax.experimental.pallas import tpu as pltpu

F32 = jnp.float32
BF16 = jnp.bfloat16
HI = lax.Precision.HIGHEST

D_MODEL = 1024
HEAD_DIM = 64
GRID_W = 64
D_RWKV = 512
D_NA = 512
H_RWKV = D_RWKV // HEAD_DIM
H_NA = D_NA // HEAD_DIM
R_DECAY = 64
R_AAA = 64
R_GATE = 128
D_SHIFT = 3 * D_RWKV + 2 * R_DECAY + 2 * R_AAA + R_GATE
N_QKV = 3 * D_NA
N_EXPERTS = 8
MOE_BLOCK = 256
RMS_EPS = 1e-6
GN_EPS = 64e-5
L2_EPS = 1e-12
NEG = -1e30

TM = 256
NA_WIN_ROWS = 12
N_FEAT = 9
TC_SCAN = 16
LANES = 128


def _cparams(sem, vmem_mb=48):
    return pltpu.CompilerParams(dimension_semantics=sem, vmem_limit_bytes=vmem_mb * 1024 * 1024)


def _sigmoid(x):
    return 1.0 / (1.0 + jnp.exp(-x))


def _silu(x):
    return x * _sigmoid(x)


def _softplus(x):
    return jnp.maximum(x, 0.0) + jnp.log(1.0 + jnp.exp(-jnp.abs(x)))


def _rms(x):
    return x * lax.rsqrt(jnp.mean(x * x, axis=-1, keepdims=True) + RMS_EPS)


def _norm_mod(x, g, scale, shift):
    return _rms(x) * g * (1.0 + scale) + shift


def _head_sum(x, bd):
    hi = x.astype(BF16)
    lo = (x - hi.astype(F32)).astype(BF16)
    return jnp.dot(hi, bd, preferred_element_type=F32) + jnp.dot(lo, bd, preferred_element_type=F32)


def _dot_nt(a, b):
    return lax.dot_general(a, b, (((1,), (1,)), ((), ())), preferred_element_type=F32)


def _mods_kernel(c_ref, w_ref, b_ref, o_ref):
    o_ref[0] = jnp.dot(_silu(c_ref[...]), w_ref[0], precision=HI, preferred_element_type=F32) + b_ref[0]


def _mods(c_all, ada_w, ada_b):
    depth, d, n6 = ada_w.shape
    nb = c_all.shape[0]
    tn = 1536
    return pl.pallas_call(
        _mods_kernel,
        grid=(depth, n6 // tn),
        in_specs=[pl.BlockSpec((nb, d), lambda l, n: (0, 0)),
                  pl.BlockSpec((1, d, tn), lambda l, n: (l, 0, n)),
                  pl.BlockSpec((1, 1, tn), lambda l, n: (l, 0, n))],
        out_specs=pl.BlockSpec((1, nb, tn), lambda l, n: (l, 0, n)),
        out_shape=jax.ShapeDtypeStruct((depth, nb, n6), F32),
        compiler_params=_cparams(("arbitrary", "arbitrary")),
        name="adaln_mods",
    )(c_all, ada_w, ada_b.reshape(depth, 1, n6))


def _mod_spec(n_ctx_tiles):
    return pl.BlockSpec((None, None, 6, D_MODEL), lambda b, j: (b, jnp.where(j >= n_ctx_tiles, 1, 0), 0, 0))


def _inproj_kernel(x_ref, mod_ref, g_ref, w_ref, qkv_ref, u_ref):
    h = _norm_mod(x_ref[0], g_ref[...], mod_ref[1:2, :], mod_ref[0:1, :]).astype(BF16)
    qkv_ref[0] = jnp.dot(h, w_ref[:, :N_QKV], preferred_element_type=F32).astype(BF16)
    u_ref[0] = jnp.dot(h, w_ref[:, N_QKV:], preferred_element_type=F32)


def _inproj(xs, modtab, g, w_in_bf, n_ctx_tiles):
    b, s, d = xs.shape
    nj = s // TM
    return pl.pallas_call(
        _inproj_kernel,
        grid=(b, nj),
        in_specs=[pl.BlockSpec((1, TM, d), lambda b, j: (b, j, 0)),
                  _mod_spec(n_ctx_tiles),
                  pl.BlockSpec((1, d), lambda b, j: (0, 0)),
                  pl.BlockSpec((d, N_QKV + D_SHIFT), lambda b, j: (0, 0))],
        out_specs=[pl.BlockSpec((1, TM, N_QKV), lambda b, j: (b, j, 0)),
                   pl.BlockSpec((1, TM, D_SHIFT), lambda b, j: (b, j, 0))],
        out_shape=[jax.ShapeDtypeStruct((b, s, N_QKV), BF16),
                   jax.ShapeDtypeStruct((b, s, D_SHIFT), F32)],
        compiler_params=_cparams(("arbitrary", "arbitrary")),
        name="in_proj",
    )(xs, modtab, g.reshape(1, d), w_in_bf)


def _na_bias_table(rpb):
    nh = rpb.shape[0]
    cq = np.arange(GRID_W)[:, None]
    ck = np.arange(GRID_W)[None, :]
    c0 = np.clip(cq - 8, 0, GRID_W - 16)
    col_ok = (ck >= c0) & (ck < c0 + 16)
    col_off = np.clip(ck - cq + 15, 0, 30)
    t = jnp.take(rpb, jnp.asarray(col_off.reshape(-1), jnp.int32), axis=2).reshape(nh, 15, GRID_W, GRID_W)
    t = jnp.where(col_ok[None, None], t, NEG)
    tabs = []
    for case in range(3):
        per_row = []
        for i in range(4):
            lo, ro = ((0, 7 - i), (i, 3), (4, 3 - i))[case]
            blk = jnp.pad(t[:, ro:ro + 8], ((0, 0), (lo, NA_WIN_ROWS - 8 - lo), (0, 0), (0, 0)),
                          constant_values=NEG)
            per_row.append(blk.transpose(0, 2, 1, 3))
        tabs.append(jnp.stack(per_row, axis=1).reshape(nh, TM, NA_WIN_ROWS * GRID_W))
    return jnp.stack(tabs, axis=1)


def _na_kernel(q_ref, k_ref, v_ref, bias_ref, o_ref, *, n_ctx, rows):
    j = pl.program_id(1)
    n_ctx_tiles = n_ctx // TM
    scale = HEAD_DIM ** -0.5
    nwin = NA_WIN_ROWS * GRID_W

    def head(h):
        return slice(h * HEAD_DIM, (h + 1) * HEAD_DIM)

    @pl.when(j < n_ctx_tiles)
    def _ctx():
        for h in range(H_NA):
            q = q_ref[0, :, head(h)]
            s = _dot_nt(q, k_ref[0, 0:n_ctx, head(h)]) * scale
            p = jnp.exp(s - jnp.max(s, axis=-1, keepdims=True))
            y = jnp.dot(p.astype(BF16), v_ref[0, 0:n_ctx, head(h)], preferred_element_type=F32)
            o_ref[0, :, head(h)] = (y / jnp.sum(p, axis=-1, keepdims=True)).astype(BF16)

    @pl.when(j >= n_ctx_tiles)
    def _lat():
        m = j - n_ctx_tiles
        w0 = jnp.clip(4 * m - 4, 0, rows - NA_WIN_ROWS)
        start = pl.multiple_of(n_ctx + w0 * GRID_W, GRID_W)
        for h in range(H_NA):
            q = q_ref[0, :, head(h)]
            s_loc = _dot_nt(q, k_ref[0, pl.ds(start, nwin), head(h)]) * scale + bias_ref[h]
            s_ctx = _dot_nt(q, k_ref[0, 0:n_ctx, head(h)]) * scale
            mx = jnp.maximum(jnp.max(s_loc, axis=-1, keepdims=True), jnp.max(s_ctx, axis=-1, keepdims=True))
            p_loc = jnp.exp(s_loc - mx)
            p_ctx = jnp.exp(s_ctx - mx)
            den = jnp.sum(p_loc, axis=-1, keepdims=True) + jnp.sum(p_ctx, axis=-1, keepdims=True)
            y = (jnp.dot(p_loc.astype(BF16), v_ref[0, pl.ds(start, nwin), head(h)], preferred_element_type=F32)
                 + jnp.dot(p_ctx.astype(BF16), v_ref[0, 0:n_ctx, head(h)], preferred_element_type=F32))
            o_ref[0, :, head(h)] = (y / den).astype(BF16)


def _na(qkv, bias_tab, n_ctx):
    b, s, _ = qkv.shape
    nj = s // TM
    n_ctx_tiles = n_ctx // TM
    rows = (s - n_ctx) // GRID_W
    n_lat_tiles = nj - n_ctx_tiles
    assert rows >= NA_WIN_ROWS and rows % 4 == 0 and n_lat_tiles >= 3

    def case_map(b, j):
        m = j - n_ctx_tiles
        return (0, jnp.where(m <= 0, 0, jnp.where(m >= n_lat_tiles - 1, 2, 1)), 0, 0)

    return pl.pallas_call(
        functools.partial(_na_kernel, n_ctx=n_ctx, rows=rows),
        grid=(b, nj),
        in_specs=[pl.BlockSpec((1, TM, D_NA), lambda b, j: (b, j, 0)),
                  pl.BlockSpec((1, s, D_NA), lambda b, j: (b, 0, 1)),
                  pl.BlockSpec((1, s, D_NA), lambda b, j: (b, 0, 2)),
                  pl.BlockSpec((H_NA, None, TM, NA_WIN_ROWS * GRID_W), case_map)],
        out_specs=pl.BlockSpec((1, TM, D_NA), lambda b, j: (b, j, 0)),
        out_shape=jax.ShapeDtypeStruct((b, s, D_NA), BF16),
        compiler_params=_cparams(("arbitrary", "arbitrary")),
        name="na_attention",
    )(qkv, qkv, qkv, bias_tab)


def _feat_kernel(u_ref, up_ref, un_ref, mu_ref, w0_ref, w2_ref, a0_ref, a2_ref, g2_ref, kk_ref, ka_ref, rk_ref,
                 bd_ref, feat_ref, rd_ref, *, n_ctx_tiles, nj):
    j = pl.program_id(1)
    seg_start = (j == 0) | (j == n_ctx_tiles)
    seg_end = (j == n_ctx_tiles - 1) | (j == nj - 1)
    rid = lax.broadcasted_iota(jnp.int32, (TM, 1), 0)

    def shifted(lo, hi):
        p = u_ref[0, :, lo:hi]
        prev_row = jnp.where(seg_start, 0.0, up_ref[0, 7:8, lo:hi])
        next_row = jnp.where(seg_end, 0.0, un_ref[0, 0:1, lo:hi])
        prev = jnp.where(rid == 0, prev_row, pltpu.roll(p, 1, axis=0))
        nxt = jnp.where(rid == TM - 1, next_row, pltpu.roll(p, TM - 1, axis=0))
        return p + mu_ref[:, lo:hi] * (0.5 * (prev + nxt) - p)

    c = D_RWKV
    r = shifted(0, c)
    k = shifted(c, 2 * c)
    v = shifted(2 * c, 3 * c)
    o = 3 * c
    wl = shifted(o, o + 2 * R_DECAY)
    al = shifted(o + 2 * R_DECAY, o + 2 * R_DECAY + 2 * R_AAA)
    gl = shifted(o + 2 * R_DECAY + 2 * R_AAA, D_SHIFT)
    bd = bd_ref[...]

    lw = jnp.dot(jnp.tanh(wl), w2_ref[...], precision=HI, preferred_element_type=F32)
    la = jnp.dot(al, a2_ref[...], precision=HI, preferred_element_type=F32)
    g = jnp.dot(_sigmoid(gl), g2_ref[...], precision=HI, preferred_element_type=F32)

    kks = k * kk_ref[...]
    kk = kks / jnp.maximum(jnp.sqrt(_head_sum(kks * kks, bd)), L2_EPS)
    feat_ref[0, :, 0:c] = r
    feat_ref[0, :, c:2 * c] = v
    feat_ref[0, :, 2 * c:3 * c] = kk
    ksum = jnp.zeros_like(k)
    for z in range(2):
        w_log = -_softplus(-(w0_ref[z:z + 1, :] + lw[:, z * c:(z + 1) * c])) - 0.5
        decay = jnp.exp(-jnp.exp(w_log))
        a = _sigmoid(a0_ref[z:z + 1, :] + la[:, z * c:(z + 1) * c])
        k_dir = k * (1.0 + (a - 1.0) * ka_ref[...])
        ksum = ksum + k_dir
        base = (3 + 3 * z) * c
        feat_ref[0, :, base:base + c] = decay
        feat_ref[0, :, base + c:base + 2 * c] = k_dir
        feat_ref[0, :, base + 2 * c:base + 3 * c] = kk * a
    bonus = _head_sum(r * (0.5 * ksum) * rk_ref[...], bd) * v
    rd_ref[0, :, 0:c] = bonus
    rd_ref[0, :, c:2 * c] = g


def _features(u_raw, n_ctx, shift_mu, w0, w2blk, a0, a2blk, g2, k_k, k_a, r_k, bd):
    b, s, _ = u_raw.shape
    nj = s // TM
    n_ctx_tiles = n_ctx // TM
    c = D_RWKV
    t8 = TM // 8

    def full(shape):
        return pl.BlockSpec(shape, lambda b, j: (0,) * len(shape))

    return pl.pallas_call(
        functools.partial(_feat_kernel, n_ctx_tiles=n_ctx_tiles, nj=nj),
        grid=(b, nj),
        in_specs=[pl.BlockSpec((1, TM, D_SHIFT), lambda b, j: (b, j, 0)),
                  pl.BlockSpec((1, 8, D_SHIFT), lambda b, j: (b, jnp.maximum(j * t8 - 1, 0), 0)),
                  pl.BlockSpec((1, 8, D_SHIFT), lambda b, j: (b, jnp.minimum((j + 1) * t8, s // 8 - 1), 0)),
                  full((1, D_SHIFT)), full((2, c)), full((2 * R_DECAY, 2 * c)), full((2, c)),
                  full((2 * R_AAA, 2 * c)), full((R_GATE, c)), full((1, c)), full((1, c)), full((1, c)),
                  full((c, c))],
        out_specs=[pl.BlockSpec((1, TM, N_FEAT * c), lambda b, j: (b, j, 0)),
                   pl.BlockSpec((1, TM, 2 * c), lambda b, j: (b, j, 0))],
        out_shape=[jax.ShapeDtypeStruct((b, s, N_FEAT * c), F32),
                   jax.ShapeDtypeStruct((b, s, 2 * c), F32)],
        compiler_params=_cparams(("arbitrary", "arbitrary")),
        name="rwkv_features",
    )(u_raw, u_raw, u_raw, shift_mu.reshape(1, D_SHIFT), w0, w2blk, a0, a2blk, g2, k_k.reshape(1, c),
      k_a.reshape(1, c), r_k.reshape(1, c), bd)


def _scan_kernel(fs_ref, fd_ref, bs_ref, bd_ref, yf_ref, yb_ref, st_ref, p_ref, op_ref):
    @pl.when(pl.program_id(0) == 0)
    def _init():
        st_ref[...] = jnp.zeros_like(st_ref)

    kg = 8
    n_groups = HEAD_DIM // kg
    p_ref[...] = jnp.ones_like(p_ref)

    def tile(ref, row, a):
        return ref[row, a].reshape(HEAD_DIM, -1)

    def one_step(d, s_ref, d_ref, y_ref, row):
        v_t = tile(s_ref, row, 1)
        p_prev = p_ref[d]
        p_new = p_prev * tile(d_ref, row, 0)
        inv = 1.0 / p_new
        p_ref[d] = p_new
        op_ref[d, 0] = (tile(s_ref, row, 2) * p_prev).reshape(n_groups, kg, -1)
        op_ref[d, 1] = (tile(d_ref, row, 2) * inv).reshape(n_groups, kg, -1)
        op_ref[d, 2] = (tile(d_ref, row, 1) * inv).reshape(n_groups, kg, -1)
        op_ref[d, 3] = (tile(s_ref, row, 0) * p_new).reshape(n_groups, kg, -1)

        sa = jnp.zeros_like(v_t)
        for k in range(HEAD_DIM):
            sa = sa + st_ref[d, k] * op_ref[d, 0, k // kg, pl.ds(k % kg, 1), :]

        def pass_update(g, y):
            for kk in range(kg):
                k = g * kg + kk
                z_new = st_ref[d, k] - sa * op_ref[d, 1, g, pl.ds(kk, 1), :] + v_t * op_ref[d, 2, g, pl.ds(kk, 1), :]
                st_ref[d, k] = z_new
                y = y + z_new * op_ref[d, 3, g, pl.ds(kk, 1), :]
            return y

        y_ref[row] = lax.fori_loop(0, n_groups, pass_update, jnp.zeros_like(v_t))

    def step(jj, carry):
        one_step(0, fs_ref, fd_ref, yf_ref, jj)
        one_step(1, bs_ref, bd_ref, yb_ref, TC_SCAN - 1 - jj)
        return carry

    lax.fori_loop(0, TC_SCAN, step, 0)

    for d in range(2):
        for k in range(HEAD_DIM):
            st_ref[d, k] = st_ref[d, k] * p_ref[d, pl.ds(k, 1), :]


def _scan(feat_t, n_ctx):
    s, _, hd, lanes = feat_t.shape
    feat_t = feat_t.reshape(s, N_FEAT, hd // 8, 8, lanes)
    nblk = s // TC_SCAN
    nc = n_ctx // TC_SCAN
    assert n_ctx % TC_SCAN == 0 and s % TC_SCAN == 0

    def bwd(i):
        return jnp.where(i < nc, nc - 1 - i, nblk - 1 - i + nc)

    blk = (TC_SCAN, 3, hd // 8, 8, lanes)
    yblk = (TC_SCAN, hd, lanes)
    return pl.pallas_call(
        _scan_kernel,
        grid=(nblk,),
        in_specs=[pl.BlockSpec(blk, lambda i: (i, 0, 0, 0, 0)),
                  pl.BlockSpec(blk, lambda i: (i, 1, 0, 0, 0)),
                  pl.BlockSpec(blk, lambda i: (bwd(i), 0, 0, 0, 0)),
                  pl.BlockSpec(blk, lambda i: (bwd(i), 2, 0, 0, 0))],
        out_specs=[pl.BlockSpec(yblk, lambda i: (i, 0, 0)),
                   pl.BlockSpec(yblk, lambda i: (bwd(i), 0, 0))],
        out_shape=[jax.ShapeDtypeStruct((s, hd, lanes), F32)] * 2,
        scratch_shapes=[pltpu.VMEM((2, HEAD_DIM, hd, lanes), F32),
                        pltpu.VMEM((2, hd, lanes), F32),
                        pltpu.VMEM((2, 4, hd // 8, 8, lanes), F32)],
        compiler_params=_cparams(("arbitrary",)),
        name="rwkv_scan",
    )(feat_t, feat_t, feat_t, feat_t)


def _outproj_kernel(yf_ref, yb_ref, rd_ref, yna_ref, x_ref, mod_ref, lnw_ref, lnb_ref, bd_ref, w_ref, o_ref):
    c = D_RWKV
    bd = bd_ref[...]
    y = yf_ref[0] + yb_ref[0]
    mu = _head_sum(y, bd) * (1.0 / HEAD_DIM)
    dlt = y - mu
    var = _head_sum(dlt * dlt, bd) * (1.0 / HEAD_DIM)
    yn = dlt * lax.rsqrt(var + GN_EPS) * lnw_ref[...] + lnb_ref[...]
    y_rw = ((yn + rd_ref[0, :, 0:c]) * rd_ref[0, :, c:2 * c]).astype(BF16)
    o = (jnp.dot(y_rw, w_ref[0:c, :], preferred_element_type=F32)
         + jnp.dot(yna_ref[0], w_ref[c:, :], preferred_element_type=F32))
    o_ref[0] = x_ref[0] + mod_ref[2:3, :] * o


def _outproj(yf, yb, rd, yna, xs, modtab, ln_w, ln_b, bd, w_out_bf, n_ctx_tiles):
    b, s, d = xs.shape
    nj = s // TM
    c = D_RWKV

    def tile(w):
        return pl.BlockSpec((1, TM, w), lambda b, j: (b, j, 0))

    def full(shape):
        return pl.BlockSpec(shape, lambda b, j: (0,) * len(shape))

    return pl.pallas_call(
        _outproj_kernel,
        grid=(b, nj),
        in_specs=[tile(c), tile(c), tile(2 * c), tile(D_NA), tile(d), _mod_spec(n_ctx_tiles),
                  full((1, c)), full((1, c)), full((c, c)), full((d, d))],
        out_specs=tile(d),
        out_shape=jax.ShapeDtypeStruct((b, s, d), F32),
        compiler_params=_cparams(("arbitrary", "arbitrary")),
        name="out_proj",
    )(yf, yb, rd, yna, xs, modtab, ln_w.reshape(1, c), ln_b.reshape(1, c), bd, w_out_bf)


def _ffn_kernel(x_ref, mod_ref, g_ref, w1_ref, w3_ref, w2_ref, o_ref, *, ffc):
    x = x_ref[0]
    h = _norm_mod(x, g_ref[...], mod_ref[4:5, :], mod_ref[3:4, :]).astype(BF16)
    d_ff = w1_ref.shape[1]
    acc = jnp.zeros(x.shape, F32)
    for c0 in range(0, d_ff, ffc):
        a = jnp.dot(h, w1_ref[:, c0:c0 + ffc], preferred_element_type=F32)
        g = jnp.dot(h, w3_ref[:, c0:c0 + ffc], preferred_element_type=F32)
        acc = acc + jnp.dot((_silu(a) * g).astype(BF16), w2_ref[c0:c0 + ffc, :], preferred_element_type=F32)
    o_ref[0] = x + mod_ref[5:6, :] * acc


def _ffn(xs, modtab, g, w1, w3, w2, n_ctx_tiles):
    b, s, d = xs.shape
    nj = s // TM
    d_ff = w1.shape[1]
    const = lambda b, j: (0, 0)
    return pl.pallas_call(
        functools.partial(_ffn_kernel, ffc=d_ff // 2),
        grid=(b, nj),
        in_specs=[pl.BlockSpec((1, TM, d), lambda b, j: (b, j, 0)), _mod_spec(n_ctx_tiles),
                  pl.BlockSpec((1, d), const),
                  pl.BlockSpec((d, d_ff), const), pl.BlockSpec((d, d_ff), const), pl.BlockSpec((d_ff, d), const)],
        out_specs=pl.BlockSpec((1, TM, d), lambda b, j: (b, j, 0)),
        out_shape=jax.ShapeDtypeStruct((b, s, d), F32),
        compiler_params=_cparams(("arbitrary", "arbitrary"), 56),
        name="ffn_swiglu",
    )(xs, modtab, g.reshape(1, d), w1, w3, w2)


def _router_kernel(x_ref, mod_ref, g_ref, r_ref, h_ref, slab_ref):
    h = _norm_mod(x_ref[0], g_ref[...], mod_ref[4:5, :], mod_ref[3:4, :])
    h_ref[0] = h
    logits = jnp.dot(h, r_ref[...], precision=HI, preferred_element_type=F32)
    lane = lax.broadcasted_iota(jnp.int32, logits.shape, 1)
    lg = jnp.where(lane < N_EXPERTS, logits, -jnp.inf)
    m1 = jnp.max(lg, axis=-1, keepdims=True)
    i1 = jnp.min(jnp.where(lg == m1, lane, LANES), axis=-1, keepdims=True)
    lg2 = jnp.where(lane == i1, -jnp.inf, lg)
    m2 = jnp.max(lg2, axis=-1, keepdims=True)
    i2 = jnp.min(jnp.where(lg2 == m2, lane, LANES), axis=-1, keepdims=True)
    e = jnp.exp(m2 - m1)
    g1 = 1.0 / (1.0 + e)
    g2 = e / (1.0 + e)
    slab_ref[0] = jnp.where(lane == 0, i1.astype(F32),
                            jnp.where(lane == 1, i2.astype(F32),
                                      jnp.where(lane == 2, g1, jnp.where(lane == 3, g2, 0.0))))


def _router(xs, modtab, g, router_pad, n_ctx_tiles):
    b, s, d = xs.shape
    nj = s // TM
    const = lambda b, j: (0, 0)
    return pl.pallas_call(
        _router_kernel,
        grid=(b, nj),
        in_specs=[pl.BlockSpec((1, TM, d), lambda b, j: (b, j, 0)), _mod_spec(n_ctx_tiles),
                  pl.BlockSpec((1, d), const), pl.BlockSpec((d, LANES), const)],
        out_specs=[pl.BlockSpec((1, TM, d), lambda b, j: (b, j, 0)),
                   pl.BlockSpec((1, TM, LANES), lambda b, j: (b, j, 0))],
        out_shape=[jax.ShapeDtypeStruct((b, s, d), F32), jax.ShapeDtypeStruct((b, s, LANES), F32)],
        compiler_params=_cparams(("arbitrary", "arbitrary")),
        name="moe_router",
    )(xs, modtab, g.reshape(1, d), router_pad)


def _dispatch(idx):
    n = idx.shape[0]
    a = n * 2
    flat_e = idx.reshape(a)
    onehot = (flat_e[None, :] == jnp.arange(N_EXPERTS, dtype=jnp.int32)[:, None]).astype(jnp.int32)
    csum = jnp.cumsum(onehot, axis=1)
    counts = csum[:, -1]
    rank = jnp.sum(csum * onehot, axis=0) - 1
    padded = (counts + MOE_BLOCK - 1) // MOE_BLOCK * MOE_BLOCK
    pad_ends = jnp.cumsum(padded)
    pad_starts = pad_ends - padded
    dest = (pad_starts[flat_e] + rank).astype(jnp.int32)
    n_blocks = -(-a // MOE_BLOCK) + N_EXPERTS
    slot_tok = jnp.zeros((n_blocks * MOE_BLOCK,), jnp.int32).at[dest].set(jnp.arange(a, dtype=jnp.int32) // 2)
    block_start = jnp.arange(n_blocks, dtype=jnp.int32) * MOE_BLOCK
    block_e = jnp.minimum(jnp.sum((pad_ends[None, :] <= block_start[:, None]).astype(jnp.int32), axis=1),
                          N_EXPERTS - 1)
    n_used = (pad_ends[-1] // MOE_BLOCK).astype(jnp.int32).reshape(1)
    return slot_tok, block_e, n_used, dest


def _moe_mm_kernel(be_ref, st_ref, nu_ref, h_hbm, w1_hbm, w3_hbm, w2_hbm, o_ref, xbuf, w1s, w3s, w2s, gsem, wsem,
                   *, ffc):
    i = pl.program_id(0)
    n_used = nu_ref[0]

    def row_copy(tok, slot, r):
        return pltpu.make_async_copy(h_hbm.at[pl.ds(tok, 1)], xbuf.at[slot, pl.ds(r, 1)], gsem.at[slot])

    def issue(blk, slot):
        def body(r, carry):
            row_copy(st_ref[blk * MOE_BLOCK + r], slot, r).start()
            return carry
        lax.fori_loop(0, MOE_BLOCK, body, 0, unroll=8)

    def wait_rows(slot):
        for r in range(MOE_BLOCK):
            row_copy(0, slot, r).wait()

    n_blocks = pl.num_programs(0)
    slot = lax.rem(i, 2)
    nslot = 1 - slot
    nxt = jnp.minimum(i + 1, n_blocks - 1)

    @pl.when(i == 0)
    def _first():
        issue(0, 0)

    @pl.when(i < n_used)
    def _compute():
        e = be_ref[i]
        changed = (i == 0) | (e != be_ref[jnp.maximum(i - 1, 0)])

        @pl.when(changed)
        def _load_weights():
            copies = [pltpu.make_async_copy(w1_hbm.at[e], w1s, wsem.at[0]),
                      pltpu.make_async_copy(w3_hbm.at[e], w3s, wsem.at[1]),
                      pltpu.make_async_copy(w2_hbm.at[e], w2s, wsem.at[2])]
            for cp in copies:
                cp.start()
            for cp in copies:
                cp.wait()

        wait_rows(slot)
        xb = xbuf[slot].astype(BF16)
        acc = jnp.zeros(o_ref.shape, F32)
        d_ff = w1s.shape[1]
        n_chunks = d_ff // ffc
        rows_per_chunk = -(-MOE_BLOCK // n_chunks)
        for ci in range(n_chunks):
            c0 = ci * ffc
            a = jnp.dot(xb, w1s[:, c0:c0 + ffc], preferred_element_type=F32)
            g = jnp.dot(xb, w3s[:, c0:c0 + ffc], preferred_element_type=F32)
            acc = acc + jnp.dot((_silu(a) * g).astype(BF16), w2s[c0:c0 + ffc, :], preferred_element_type=F32)
            for r in range(ci * rows_per_chunk, min((ci + 1) * rows_per_chunk, MOE_BLOCK)):
                row_copy(st_ref[nxt * MOE_BLOCK + r], nslot, r).start()
        o_ref[...] = acc

    @pl.when(i >= n_used)
    def _unused():
        o_ref[...] = jnp.zeros_like(o_ref)
        wait_rows(slot)
        issue(nxt, nslot)

    @pl.when(i == n_blocks - 1)
    def _drain():
        wait_rows(nslot)


def _moe_mm(h_flat, slot_tok, block_e, n_used, w1, w3, w2):
    n, d = h_flat.shape
    n_blocks = block_e.shape[0]
    d_ff = w1.shape[2]
    any_spec = pl.BlockSpec(memory_space=pl.ANY)
    return pl.pallas_call(
        functools.partial(_moe_mm_kernel, ffc=512),
        grid_spec=pltpu.PrefetchScalarGridSpec(
            num_scalar_prefetch=3,
            grid=(n_blocks,),
            in_specs=[any_spec, any_spec, any_spec, any_spec],
            out_specs=pl.BlockSpec((MOE_BLOCK, d), lambda i, be, st, nu: (i, 0)),
            scratch_shapes=[pltpu.VMEM((2, MOE_BLOCK, d), F32),
                            pltpu.VMEM((d, d_ff), BF16), pltpu.VMEM((d, d_ff), BF16), pltpu.VMEM((d_ff, d), BF16),
                            pltpu.SemaphoreType.DMA((2,)), pltpu.SemaphoreType.DMA((3,))]),
        out_shape=jax.ShapeDtypeStruct((n_blocks * MOE_BLOCK, d), F32),
        compiler_params=_cparams(("arbitrary",), 56),
        name="moe_grouped_swiglu",
    )(block_e, slot_tok, n_used, h_flat, w1, w3, w2)


def _moe_comb_kernel(inv_ref, x_ref, mod_ref, slab_ref, fg_ref, yb_hbm, o_ref, gbuf, gsem, *, nj, final):
    b = pl.program_id(0)
    j = pl.program_id(1)
    t = b * nj + j
    nt = pl.num_programs(0) * nj

    def row_copy(src, slot, which, r):
        return pltpu.make_async_copy(yb_hbm.at[pl.ds(src, 1)], gbuf.at[slot, which, pl.ds(r, 1)], gsem.at[slot])

    def issue(tile, slot):
        def body(r, carry):
            base = (tile * TM + r) * 2
            row_copy(inv_ref[base], slot, 0, r).start()
            row_copy(inv_ref[base + 1], slot, 1, r).start()
            return carry
        lax.fori_loop(0, TM, body, 0, unroll=8)

    def wait_rows(slot):
        for r in range(TM):
            row_copy(0, slot, 0, r).wait()
            row_copy(0, slot, 1, r).wait()

    @pl.when(t == 0)
    def _first():
        issue(0, 0)

    @pl.when(t + 1 < nt)
    def _prefetch():
        issue(t + 1, lax.rem(t + 1, 2))

    slot = lax.rem(t, 2)
    wait_rows(slot)
    slab = slab_ref[0]
    y = slab[:, 2:3] * gbuf[slot, 0] + slab[:, 3:4] * gbuf[slot, 1]
    xn = x_ref[0] + mod_ref[5:6, :] * y
    if final:
        xn = _rms(xn) * fg_ref[...]
    o_ref[0] = xn


def _moe_combine(inv, xs, modtab, slab, final_g, yb, n_ctx_tiles, final):
    b, s, d = xs.shape
    nj = s // TM
    return pl.pallas_call(
        functools.partial(_moe_comb_kernel, nj=nj, final=final),
        grid_spec=pltpu.PrefetchScalarGridSpec(
            num_scalar_prefetch=1,
            grid=(b, nj),
            in_specs=[pl.BlockSpec((1, TM, d), lambda b, j, inv: (b, j, 0)),
                      pl.BlockSpec((None, None, 6, D_MODEL),
                                   lambda b, j, inv: (b, jnp.where(j >= n_ctx_tiles, 1, 0), 0, 0)),
                      pl.BlockSpec((1, TM, LANES), lambda b, j, inv: (b, j, 0)),
                      pl.BlockSpec((1, d), lambda b, j, inv: (0, 0)),
                      pl.BlockSpec(memory_space=pl.ANY)],
            out_specs=pl.BlockSpec((1, TM, d), lambda b, j, inv: (b, j, 0)),
            scratch_shapes=[pltpu.VMEM((2, 2, TM, d), F32), pltpu.SemaphoreType.DMA((2,))]),
        out_shape=jax.ShapeDtypeStruct((b, s, d), F32),
        compiler_params=_cparams(("arbitrary", "arbitrary")),
        name="moe_combine",
    )(inv, xs, modtab, slab, final_g.reshape(1, d), yb)


def _blockdiag(w):
    z = jnp.zeros_like(w[0])
    return jnp.concatenate([jnp.concatenate([w[0], z], axis=1), jnp.concatenate([z, w[1]], axis=1)], axis=0)


def kernel(x, c, ctx, c_ctx, ada_w, ada_b, norm_mix_g, norm_ffn_g, w_in, shift_mu, w0, w2, a0, a2, g2, k_k, k_a, r_k,
           ln_x_w, ln_x_b, na_rpb, w_out, ffn_w1, ffn_w3, ffn_w2, router, moe_w1, moe_w3, moe_w2, final_g):
    b, t, d = x.shape
    n_ctx = ctx.shape[1]
    s = n_ctx + t
    depth = ada_w.shape[0]
    assert d == D_MODEL and n_ctx % TM == 0 and t % TM == 0
    n_ctx_tiles = n_ctx // TM
    lanes = b * H_RWKV

    nb = -(-(b + 1) // 8) * 8
    c_all = jnp.concatenate([c, c_ctx[None, :], jnp.zeros((nb - b - 1, d), F32)], axis=0)
    mods = _mods(c_all, ada_w, ada_b).reshape(depth, nb, 6, d)
    modtab = jnp.stack([jnp.broadcast_to(mods[:, b:b + 1], (depth, b, 6, d)), mods[:, :b]], axis=2)

    head_id = np.arange(D_RWKV) // HEAD_DIM
    bd = jnp.asarray(head_id[:, None] == head_id[None, :], BF16)

    xs = jnp.concatenate([ctx, x], axis=1)
    for l in range(depth):
        last = l == depth - 1
        qkv, u_raw = _inproj(xs, modtab[l], norm_mix_g[l], w_in[l].astype(BF16), n_ctx_tiles)
        y_na = _na(qkv, _na_bias_table(na_rpb[l]), n_ctx)
        feat, rd = _features(u_raw, n_ctx, shift_mu[l], w0[l], _blockdiag(w2[l]), a0[l], _blockdiag(a2[l]), g2[l],
                             k_k[l], k_a[l], r_k[l], bd)
        feat_t = feat.reshape(b, s, N_FEAT, H_RWKV, HEAD_DIM).transpose(1, 2, 4, 0, 3).reshape(
            s, N_FEAT, HEAD_DIM, lanes)
        yf_t, yb_t = _scan(feat_t, n_ctx)

        def untranspose(y):
            return y.reshape(s, HEAD_DIM, b, H_RWKV).transpose(2, 0, 3, 1).reshape(b, s, D_RWKV)

        xs = _outproj(untranspose(yf_t), untranspose(yb_t), rd, y_na, xs, modtab[l], ln_x_w[l], ln_x_b[l], bd,
                      w_out[l].astype(BF16), n_ctx_tiles)
        i = l // 2
        if l % 2 == 0:
            xs = _ffn(xs, modtab[l], norm_ffn_g[l], ffn_w1[i].astype(BF16), ffn_w3[i].astype(BF16),
                      ffn_w2[i].astype(BF16), n_ctx_tiles)
        else:
            xm, nct = (xs[:, n_ctx:], 0) if last else (xs, n_ctx_tiles)
            n_tok = xm.shape[0] * xm.shape[1]
            router_pad = jnp.pad(router[i], ((0, 0), (0, LANES - N_EXPERTS)))
            h, slab = _router(xm, modtab[l], norm_ffn_g[l], router_pad, nct)
            idx = slab[..., 0:2].astype(jnp.int32).reshape(n_tok, 2)
            slot_tok, block_e, n_used, dest = _dispatch(idx)
            yb = _moe_mm(h.reshape(n_tok, d), slot_tok, block_e, n_used, moe_w1[i].astype(BF16),
                         moe_w3[i].astype(BF16), moe_w2[i].astype(BF16))
            xs = _moe_combine(dest, xm, modtab[l], slab, final_g, yb, nct, final=last)
    if depth % 2 == 1:
        raise NotImplementedError("final norm is fused into the MoE combine of the last (odd) layer")
    return xs
```

```python
import functools

import numpy as np
import jax
import jax.numpy as jnp
from jax import lax
from jax.experimental import pallas as pl
from jax.experimental.pallas import tpu as pltpu

F32 = jnp.float32
BF16 = jnp.bfloat16
HI = lax.Precision.HIGHEST

D_MODEL = 1024
HEAD_DIM = 64
GRID_W = 64
D_RWKV = 512
D_NA = 512
H_RWKV = D_RWKV // HEAD_DIM
H_NA = D_NA // HEAD_DIM
R_DECAY = 64
R_AAA = 64
R_GATE = 128
D_SHIFT = 3 * D_RWKV + 2 * R_DECAY + 2 * R_AAA + R_GATE
N_QKV = 3 * D_NA
N_EXPERTS = 8
MOE_BLOCK = 256
RMS_EPS = 1e-6
GN_EPS = 64e-5
L2_EPS = 1e-12
NEG = -1e30

TM = 256
NA_WIN_ROWS = 12
N_FEAT = 9
TC_SCAN = 16
LANES = 128


def _cparams(sem, vmem_mb=48):
    return pltpu.CompilerParams(dimension_semantics=sem, vmem_limit_bytes=vmem_mb * 1024 * 1024)


def _sigmoid(x):
    return 1.0 / (1.0 + jnp.exp(-x))


def _silu(x):
    return x * _sigmoid(x)


def _softplus(x):
    return jnp.maximum(x, 0.0) + jnp.log(1.0 + jnp.exp(-jnp.abs(x)))


def _rms(x):
    return x * lax.rsqrt(jnp.mean(x * x, axis=-1, keepdims=True) + RMS_EPS)


def _norm_mod(x, g, scale, shift):
    return _rms(x) * g * (1.0 + scale) + shift


def _head_sum(x, bd):
    hi = x.astype(BF16)
    lo = (x - hi.astype(F32)).astype(BF16)
    return jnp.dot(hi, bd, preferred_element_type=F32) + jnp.dot(lo, bd, preferred_element_type=F32)


def _dot_nt(a, b):
    return lax.dot_general(a, b, (((1,), (1,)), ((), ())), preferred_element_type=F32)


def _mods_kernel(c_ref, w_ref, b_ref, o_ref):
    o_ref[0] = jnp.dot(_silu(c_ref[...]), w_ref[0], precision=HI, preferred_element_type=F32) + b_ref[0]


def _mods(c_all, ada_w, ada_b):
    depth, d, n6 = ada_w.shape
    nb = c_all.shape[0]
    tn = 1536
    return pl.pallas_call(
        _mods_kernel,
        grid=(depth, n6 // tn),
        in_specs=[pl.BlockSpec((nb, d), lambda l, n: (0, 0)),
                  pl.BlockSpec((1, d, tn), lambda l, n: (l, 0, n)),
                  pl.BlockSpec((1, 1, tn), lambda l, n: (l, 0, n))],
        out_specs=pl.BlockSpec((1, nb, tn), lambda l, n: (l, 0, n)),
        out_shape=jax.ShapeDtypeStruct((depth, nb, n6), F32),
        compiler_params=_cparams(("arbitrary", "arbitrary")),
        name="adaln_mods",
    )(c_all, ada_w, ada_b.reshape(depth, 1, n6))


def _mod_spec(n_ctx_tiles):
    return pl.BlockSpec((None, None, 6, D_MODEL), lambda b, j: (b, jnp.where(j >= n_ctx_tiles, 1, 0), 0, 0))


def _inproj_kernel(x_ref, mod_ref, g_ref, w_ref, qkv_ref, u_ref):
    h = _norm_mod(x_ref[0], g_ref[...], mod_ref[1:2, :], mod_ref[0:1, :]).astype(BF16)
    qkv_ref[0] = jnp.dot(h, w_ref[:, :N_QKV], preferred_element_type=F32).astype(BF16)
    u_ref[0] = jnp.dot(h, w_ref[:, N_QKV:], preferred_element_type=F32)


def _inproj(xs, modtab, g, w_in_bf, n_ctx_tiles):
    b, s, d = xs.shape
    nj = s // TM
    return pl.pallas_call(
        _inproj_kernel,
        grid=(b, nj),
        in_specs=[pl.BlockSpec((1, TM, d), lambda b, j: (b, j, 0)),
                  _mod_spec(n_ctx_tiles),
                  pl.BlockSpec((1, d), lambda b, j: (0, 0)),
                  pl.BlockSpec((d, N_QKV + D_SHIFT), lambda b, j: (0, 0))],
        out_specs=[pl.BlockSpec((1, TM, N_QKV), lambda b, j: (b, j, 0)),
                   pl.BlockSpec((1, TM, D_SHIFT), lambda b, j: (b, j, 0))],
        out_shape=[jax.ShapeDtypeStruct((b, s, N_QKV), BF16),
                   jax.ShapeDtypeStruct((b, s, D_SHIFT), F32)],
        compiler_params=_cparams(("arbitrary", "arbitrary")),
        name="in_proj",
    )(xs, modtab, g.reshape(1, d), w_in_bf)


def _na_bias_table(rpb):
    nh = rpb.shape[0]
    cq = np.arange(GRID_W)[:, None]
    ck = np.arange(GRID_W)[None, :]
    c0 = np.clip(cq - 8, 0, GRID_W - 16)
    col_ok = (ck >= c0) & (ck < c0 + 16)
    col_off = np.clip(ck - cq + 15, 0, 30)
    t = jnp.take(rpb, jnp.asarray(col_off.reshape(-1), jnp.int32), axis=2).reshape(nh, 15, GRID_W, GRID_W)
    t = jnp.where(col_ok[None, None], t, NEG)
    tabs = []
    for case in range(3):
        per_row = []
        for i in range(4):
            lo, ro = ((0, 7 - i), (i, 3), (4, 3 - i))[case]
            blk = jnp.pad(t[:, ro:ro + 8], ((0, 0), (lo, NA_WIN_ROWS - 8 - lo), (0, 0), (0, 0)),
                          constant_values=NEG)
            per_row.append(blk.transpose(0, 2, 1, 3))
        tabs.append(jnp.stack(per_row, axis=1).reshape(nh, TM, NA_WIN_ROWS * GRID_W))
    return jnp.stack(tabs, axis=1)


def _na_kernel(q_ref, k_ref, v_ref, bias_ref, after_ref, o_ref, *, n_ctx, rows):
    del after_ref
    j = pl.program_id(1)
    n_ctx_tiles = n_ctx // TM
    scale = HEAD_DIM ** -0.5
    nwin = NA_WIN_ROWS * GRID_W

    def head(h):
        return slice(h * HEAD_DIM, (h + 1) * HEAD_DIM)

    @pl.when(j < n_ctx_tiles)
    def _ctx():
        for h in range(H_NA):
            q = q_ref[0, :, head(h)]
            s = _dot_nt(q, k_ref[0, 0:n_ctx, head(h)]) * scale
            p = jnp.exp(s - jnp.max(s, axis=-1, keepdims=True))
            y = jnp.dot(p.astype(BF16), v_ref[0, 0:n_ctx, head(h)], preferred_element_type=F32)
            o_ref[0, :, head(h)] = (y / jnp.sum(p, axis=-1, keepdims=True)).astype(BF16)

    @pl.when(j >= n_ctx_tiles)
    def _lat():
        m = j - n_ctx_tiles
        w0 = jnp.clip(4 * m - 4, 0, rows - NA_WIN_ROWS)
        start = pl.multiple_of(n_ctx + w0 * GRID_W, GRID_W)
        for h in range(H_NA):
            q = q_ref[0, :, head(h)]
            s_loc = _dot_nt(q, k_ref[0, pl.ds(start, nwin), head(h)]) * scale + bias_ref[h]
            s_ctx = _dot_nt(q, k_ref[0, 0:n_ctx, head(h)]) * scale
            mx = jnp.maximum(jnp.max(s_loc, axis=-1, keepdims=True), jnp.max(s_ctx, axis=-1, keepdims=True))
            p_loc = jnp.exp(s_loc - mx)
            p_ctx = jnp.exp(s_ctx - mx)
            den = jnp.sum(p_loc, axis=-1, keepdims=True) + jnp.sum(p_ctx, axis=-1, keepdims=True)
            y = (jnp.dot(p_loc.astype(BF16), v_ref[0, pl.ds(start, nwin), head(h)], preferred_element_type=F32)
                 + jnp.dot(p_ctx.astype(BF16), v_ref[0, 0:n_ctx, head(h)], preferred_element_type=F32))
            o_ref[0, :, head(h)] = (y / den).astype(BF16)


def _na(qkv, bias_tab, n_ctx, after):
    b, s, _ = qkv.shape
    nj = s // TM
    n_ctx_tiles = n_ctx // TM
    rows = (s - n_ctx) // GRID_W
    n_lat_tiles = nj - n_ctx_tiles
    assert rows >= NA_WIN_ROWS and rows % 4 == 0 and n_lat_tiles >= 3

    def case_map(b, j):
        m = j - n_ctx_tiles
        return (0, jnp.where(m <= 0, 0, jnp.where(m >= n_lat_tiles - 1, 2, 1)), 0, 0)

    return pl.pallas_call(
        functools.partial(_na_kernel, n_ctx=n_ctx, rows=rows),
        grid=(b, nj),
        in_specs=[pl.BlockSpec((1, TM, D_NA), lambda b, j: (b, j, 0)),
                  pl.BlockSpec((1, s, D_NA), lambda b, j: (b, 0, 1)),
                  pl.BlockSpec((1, s, D_NA), lambda b, j: (b, 0, 2)),
                  pl.BlockSpec((H_NA, None, TM, NA_WIN_ROWS * GRID_W), case_map),
                  pl.BlockSpec((1, 8, LANES), lambda b, j: (0, 0, 0))],
        out_specs=pl.BlockSpec((1, TM, D_NA), lambda b, j: (b, j, 0)),
        out_shape=jax.ShapeDtypeStruct((b, s, D_NA), BF16),
        compiler_params=_cparams(("arbitrary", "arbitrary")),
        name="na_attention",
    )(qkv, qkv, qkv, bias_tab, after)


def _feat_kernel(u_ref, up_ref, un_ref, mu_ref, w0_ref, w2_ref, a0_ref, a2_ref, g2_ref, kk_ref, ka_ref, rk_ref,
                 bd_ref, feat_ref, rd_ref, *, n_ctx_tiles, nj):
    j = pl.program_id(1)
    seg_start = (j == 0) | (j == n_ctx_tiles)
    seg_end = (j == n_ctx_tiles - 1) | (j == nj - 1)
    rid = lax.broadcasted_iota(jnp.int32, (TM, 1), 0)

    def shifted(lo, hi):
        p = u_ref[0, :, lo:hi]
        prev_row = jnp.where(seg_start, 0.0, up_ref[0, 7:8, lo:hi])
        next_row = jnp.where(seg_end, 0.0, un_ref[0, 0:1, lo:hi])
        prev = jnp.where(rid == 0, prev_row, pltpu.roll(p, 1, axis=0))
        nxt = jnp.where(rid == TM - 1, next_row, pltpu.roll(p, TM - 1, axis=0))
        return p + mu_ref[:, lo:hi] * (0.5 * (prev + nxt) - p)

    c = D_RWKV
    r = shifted(0, c)
    k = shifted(c, 2 * c)
    v = shifted(2 * c, 3 * c)
    o = 3 * c
    wl = shifted(o, o + 2 * R_DECAY)
    al = shifted(o + 2 * R_DECAY, o + 2 * R_DECAY + 2 * R_AAA)
    gl = shifted(o + 2 * R_DECAY + 2 * R_AAA, D_SHIFT)
    bd = bd_ref[...]

    lw = jnp.dot(jnp.tanh(wl), w2_ref[...], precision=HI, preferred_element_type=F32)
    la = jnp.dot(al, a2_ref[...], precision=HI, preferred_element_type=F32)
    g = jnp.dot(_sigmoid(gl), g2_ref[...], precision=HI, preferred_element_type=F32)

    kks = k * kk_ref[...]
    kk = kks / jnp.maximum(jnp.sqrt(_head_sum(kks * kks, bd)), L2_EPS)
    feat_ref[0, :, 0:c] = r
    feat_ref[0, :, c:2 * c] = v
    feat_ref[0, :, 2 * c:3 * c] = kk
    ksum = jnp.zeros_like(k)
    for z in range(2):
        w_log = -_softplus(-(w0_ref[z:z + 1, :] + lw[:, z * c:(z + 1) * c])) - 0.5
        decay = jnp.exp(-jnp.exp(w_log))
        a = _sigmoid(a0_ref[z:z + 1, :] + la[:, z * c:(z + 1) * c])
        k_dir = k * (1.0 + (a - 1.0) * ka_ref[...])
        ksum = ksum + k_dir
        base = (3 + 3 * z) * c
        feat_ref[0, :, base:base + c] = decay
        feat_ref[0, :, base + c:base + 2 * c] = k_dir
        feat_ref[0, :, base + 2 * c:base + 3 * c] = kk * a
    bonus = _head_sum(r * (0.5 * ksum) * rk_ref[...], bd) * v
    rd_ref[0, :, 0:c] = bonus
    rd_ref[0, :, c:2 * c] = g


def _features(u_raw, n_ctx, shift_mu, w0, w2blk, a0, a2blk, g2, k_k, k_a, r_k, bd):
    b, s, _ = u_raw.shape
    nj = s // TM
    n_ctx_tiles = n_ctx // TM
    c = D_RWKV
    t8 = TM // 8

    def full(shape):
        return pl.BlockSpec(shape, lambda b, j: (0,) * len(shape))

    return pl.pallas_call(
        functools.partial(_feat_kernel, n_ctx_tiles=n_ctx_tiles, nj=nj),
        grid=(b, nj),
        in_specs=[pl.BlockSpec((1, TM, D_SHIFT), lambda b, j: (b, j, 0)),
                  pl.BlockSpec((1, 8, D_SHIFT), lambda b, j: (b, jnp.maximum(j * t8 - 1, 0), 0)),
                  pl.BlockSpec((1, 8, D_SHIFT), lambda b, j: (b, jnp.minimum((j + 1) * t8, s // 8 - 1), 0)),
                  full((1, D_SHIFT)), full((2, c)), full((2 * R_DECAY, 2 * c)), full((2, c)),
                  full((2 * R_AAA, 2 * c)), full((R_GATE, c)), full((1, c)), full((1, c)), full((1, c)),
                  full((c, c))],
        out_specs=[pl.BlockSpec((1, TM, N_FEAT * c), lambda b, j: (b, j, 0)),
                   pl.BlockSpec((1, TM, 2 * c), lambda b, j: (b, j, 0))],
        out_shape=[jax.ShapeDtypeStruct((b, s, N_FEAT * c), F32),
                   jax.ShapeDtypeStruct((b, s, 2 * c), F32)],
        compiler_params=_cparams(("arbitrary", "arbitrary")),
        name="rwkv_features",
    )(u_raw, u_raw, u_raw, shift_mu.reshape(1, D_SHIFT), w0, w2blk, a0, a2blk, g2, k_k.reshape(1, c),
      k_a.reshape(1, c), r_k.reshape(1, c), bd)


def _scan_kernel(fs_ref, fd_ref, bs_ref, bd_ref, after_ref, yf_ref, yb_ref, st_ref, p_ref, op_ref):
    del after_ref

    @pl.when(pl.program_id(0) == 0)
    def _init():
        st_ref[...] = jnp.zeros_like(st_ref)

    kg = 8
    n_groups = HEAD_DIM // kg
    p_ref[...] = jnp.ones_like(p_ref)

    def tile(ref, row, a):
        return ref[row, a].reshape(HEAD_DIM, -1)

    def one_step(d, s_ref, d_ref, y_ref, row):
        v_t = tile(s_ref, row, 1)
        p_prev = p_ref[d]
        p_new = p_prev * tile(d_ref, row, 0)
        inv = 1.0 / p_new
        p_ref[d] = p_new
        op_ref[d, 0] = (tile(s_ref, row, 2) * p_prev).reshape(n_groups, kg, -1)
        op_ref[d, 1] = (tile(d_ref, row, 2) * inv).reshape(n_groups, kg, -1)
        op_ref[d, 2] = (tile(d_ref, row, 1) * inv).reshape(n_groups, kg, -1)
        op_ref[d, 3] = (tile(s_ref, row, 0) * p_new).reshape(n_groups, kg, -1)

        sa = jnp.zeros_like(v_t)
        for k in range(HEAD_DIM):
            sa = sa + st_ref[d, k] * op_ref[d, 0, k // kg, pl.ds(k % kg, 1), :]

        def pass_update(g, y):
            for kk in range(kg):
                k = g * kg + kk
                z_new = st_ref[d, k] - sa * op_ref[d, 1, g, pl.ds(kk, 1), :] + v_t * op_ref[d, 2, g, pl.ds(kk, 1), :]
                st_ref[d, k] = z_new
                y = y + z_new * op_ref[d, 3, g, pl.ds(kk, 1), :]
            return y

        y_ref[row] = lax.fori_loop(0, n_groups, pass_update, jnp.zeros_like(v_t))

    def step(jj, carry):
        one_step(0, fs_ref, fd_ref, yf_ref, jj)
        one_step(1, bs_ref, bd_ref, yb_ref, TC_SCAN - 1 - jj)
        return carry

    lax.fori_loop(0, TC_SCAN, step, 0)

    for d in range(2):
        for k in range(HEAD_DIM):
            st_ref[d, k] = st_ref[d, k] * p_ref[d, pl.ds(k, 1), :]


def _scan(feat_t, n_ctx, after):
    s, _, hd, lanes = feat_t.shape
    feat_t = feat_t.reshape(s, N_FEAT, hd // 8, 8, lanes)
    nblk = s // TC_SCAN
    nc = n_ctx // TC_SCAN
    assert n_ctx % TC_SCAN == 0 and s % TC_SCAN == 0

    def bwd(i):
        return jnp.where(i < nc, nc - 1 - i, nblk - 1 - i + nc)

    blk = (TC_SCAN, 3, hd // 8, 8, lanes)
    yblk = (TC_SCAN, hd, lanes)
    return pl.pallas_call(
        _scan_kernel,
        grid=(nblk,),
        in_specs=[pl.BlockSpec(blk, lambda i: (i, 0, 0, 0, 0)),
                  pl.BlockSpec(blk, lambda i: (i, 1, 0, 0, 0)),
                  pl.BlockSpec(blk, lambda i: (bwd(i), 0, 0, 0, 0)),
                  pl.BlockSpec(blk, lambda i: (bwd(i), 2, 0, 0, 0)),
                  pl.BlockSpec((1, 16, after.shape[2]), lambda i: (0, 0, 0))],
        out_specs=[pl.BlockSpec(yblk, lambda i: (i, 0, 0)),
                   pl.BlockSpec(yblk, lambda i: (bwd(i), 0, 0))],
        out_shape=[jax.ShapeDtypeStruct((s, hd, lanes), F32)] * 2,
        scratch_shapes=[pltpu.VMEM((2, HEAD_DIM, hd, lanes), F32),
                        pltpu.VMEM((2, hd, lanes), F32),
                        pltpu.VMEM((2, 4, hd // 8, 8, lanes), F32)],
        compiler_params=_cparams(("arbitrary",)),
        name="rwkv_scan",
    )(feat_t, feat_t, feat_t, feat_t, after)


def _outproj_kernel(yf_ref, yb_ref, rd_ref, yna_ref, x_ref, mod_ref, lnw_ref, lnb_ref, bd_ref, w_ref, o_ref):
    c = D_RWKV
    bd = bd_ref[...]
    y = yf_ref[0] + yb_ref[0]
    mu = _head_sum(y, bd) * (1.0 / HEAD_DIM)
    dlt = y - mu
    var = _head_sum(dlt * dlt, bd) * (1.0 / HEAD_DIM)
    yn = dlt * lax.rsqrt(var + GN_EPS) * lnw_ref[...] + lnb_ref[...]
    y_rw = ((yn + rd_ref[0, :, 0:c]) * rd_ref[0, :, c:2 * c]).astype(BF16)
    o = (jnp.dot(y_rw, w_ref[0:c, :], preferred_element_type=F32)
         + jnp.dot(yna_ref[0], w_ref[c:, :], preferred_element_type=F32))
    o_ref[0] = x_ref[0] + mod_ref[2:3, :] * o


def _outproj(yf, yb, rd, yna, xs, modtab, ln_w, ln_b, bd, w_out_bf, n_ctx_tiles):
    b, s, d = xs.shape
    nj = s // TM
    c = D_RWKV

    def tile(w):
        return pl.BlockSpec((1, TM, w), lambda b, j: (b, j, 0))

    def full(shape):
        return pl.BlockSpec(shape, lambda b, j: (0,) * len(shape))

    return pl.pallas_call(
        _outproj_kernel,
        grid=(b, nj),
        in_specs=[tile(c), tile(c), tile(2 * c), tile(D_NA), tile(d), _mod_spec(n_ctx_tiles),
                  full((1, c)), full((1, c)), full((c, c)), full((d, d))],
        out_specs=tile(d),
        out_shape=jax.ShapeDtypeStruct((b, s, d), F32),
        compiler_params=_cparams(("arbitrary", "arbitrary")),
        name="out_proj",
    )(yf, yb, rd, yna, xs, modtab, ln_w.reshape(1, c), ln_b.reshape(1, c), bd, w_out_bf)


def _ffn_kernel(x_ref, mod_ref, g_ref, w1_ref, w3_ref, w2_ref, o_ref, *, ffc):
    x = x_ref[0]
    h = _norm_mod(x, g_ref[...], mod_ref[4:5, :], mod_ref[3:4, :]).astype(BF16)
    d_ff = w1_ref.shape[1]
    acc = jnp.zeros(x.shape, F32)
    for c0 in range(0, d_ff, ffc):
        a = jnp.dot(h, w1_ref[:, c0:c0 + ffc], preferred_element_type=F32)
        g = jnp.dot(h, w3_ref[:, c0:c0 + ffc], preferred_element_type=F32)
        acc = acc + jnp.dot((_silu(a) * g).astype(BF16), w2_ref[c0:c0 + ffc, :], preferred_element_type=F32)
    o_ref[0] = x + mod_ref[5:6, :] * acc


def _ffn(xs, modtab, g, w1, w3, w2, n_ctx_tiles):
    b, s, d = xs.shape
    nj = s // TM
    d_ff = w1.shape[1]
    const = lambda b, j: (0, 0)
    return pl.pallas_call(
        functools.partial(_ffn_kernel, ffc=d_ff // 2),
        grid=(b, nj),
        in_specs=[pl.BlockSpec((1, TM, d), lambda b, j: (b, j, 0)), _mod_spec(n_ctx_tiles),
                  pl.BlockSpec((1, d), const),
                  pl.BlockSpec((d, d_ff), const), pl.BlockSpec((d, d_ff), const), pl.BlockSpec((d_ff, d), const)],
        out_specs=pl.BlockSpec((1, TM, d), lambda b, j: (b, j, 0)),
        out_shape=jax.ShapeDtypeStruct((b, s, d), F32),
        compiler_params=_cparams(("arbitrary", "arbitrary"), 56),
        name="ffn_swiglu",
    )(xs, modtab, g.reshape(1, d), w1, w3, w2)


def _router_kernel(x_ref, mod_ref, g_ref, r_ref, h_ref, slab_ref):
    h = _norm_mod(x_ref[0], g_ref[...], mod_ref[4:5, :], mod_ref[3:4, :])
    h_ref[0] = h
    logits = jnp.dot(h, r_ref[...], precision=HI, preferred_element_type=F32)
    lane = lax.broadcasted_iota(jnp.int32, logits.shape, 1)
    lg = jnp.where(lane < N_EXPERTS, logits, -jnp.inf)
    m1 = jnp.max(lg, axis=-1, keepdims=True)
    i1 = jnp.min(jnp.where(lg == m1, lane, LANES), axis=-1, keepdims=True)
    lg2 = jnp.where(lane == i1, -jnp.inf, lg)
    m2 = jnp.max(lg2, axis=-1, keepdims=True)
    i2 = jnp.min(jnp.where(lg2 == m2, lane, LANES), axis=-1, keepdims=True)
    e = jnp.exp(m2 - m1)
    g1 = 1.0 / (1.0 + e)
    g2 = e / (1.0 + e)
    slab_ref[0] = jnp.where(lane == 0, i1.astype(F32),
                            jnp.where(lane == 1, i2.astype(F32),
                                      jnp.where(lane == 2, g1, jnp.where(lane == 3, g2, 0.0))))


def _router(xs, modtab, g, router_pad, n_ctx_tiles):
    b, s, d = xs.shape
    nj = s // TM
    const = lambda b, j: (0, 0)
    return pl.pallas_call(
        _router_kernel,
        grid=(b, nj),
        in_specs=[pl.BlockSpec((1, TM, d), lambda b, j: (b, j, 0)), _mod_spec(n_ctx_tiles),
                  pl.BlockSpec((1, d), const), pl.BlockSpec((d, LANES), const)],
        out_specs=[pl.BlockSpec((1, TM, d), lambda b, j: (b, j, 0)),
                   pl.BlockSpec((1, TM, LANES), lambda b, j: (b, j, 0))],
        out_shape=[jax.ShapeDtypeStruct((b, s, d), F32), jax.ShapeDtypeStruct((b, s, LANES), F32)],
        compiler_params=_cparams(("arbitrary", "arbitrary")),
        name="moe_router",
    )(xs, modtab, g.reshape(1, d), router_pad)


def _dispatch(idx):
    n = idx.shape[0]
    a = n * 2
    flat_e = idx.reshape(a)
    onehot = (flat_e[None, :] == jnp.arange(N_EXPERTS, dtype=jnp.int32)[:, None]).astype(jnp.int32)
    csum = jnp.cumsum(onehot, axis=1)
    counts = csum[:, -1]
    rank = jnp.sum(csum * onehot, axis=0) - 1
    padded = (counts + MOE_BLOCK - 1) // MOE_BLOCK * MOE_BLOCK
    pad_ends = jnp.cumsum(padded)
    pad_starts = pad_ends - padded
    dest = (pad_starts[flat_e] + rank).astype(jnp.int32)
    n_blocks = -(-a // MOE_BLOCK) + N_EXPERTS
    slot_tok = jnp.zeros((n_blocks * MOE_BLOCK,), jnp.int32).at[dest].set(jnp.arange(a, dtype=jnp.int32) // 2)
    block_start = jnp.arange(n_blocks, dtype=jnp.int32) * MOE_BLOCK
    block_e = jnp.minimum(jnp.sum((pad_ends[None, :] <= block_start[:, None]).astype(jnp.int32), axis=1),
                          N_EXPERTS - 1)
    n_used = (pad_ends[-1] // MOE_BLOCK).astype(jnp.int32).reshape(1)
    return slot_tok, block_e, n_used, dest


def _moe_mm_kernel(be_ref, st_ref, nu_ref, h_hbm, w1_hbm, w3_hbm, w2_hbm, o_ref, xbuf, w1s, w3s, w2s, gsem, wsem,
                   *, ffc):
    i = pl.program_id(0)
    n_used = nu_ref[0]

    def row_copy(tok, slot, r):
        return pltpu.make_async_copy(h_hbm.at[pl.ds(tok, 1)], xbuf.at[slot, pl.ds(r, 1)], gsem.at[slot])

    def issue(blk, slot):
        def body(r, carry):
            row_copy(st_ref[blk * MOE_BLOCK + r], slot, r).start()
            return carry
        lax.fori_loop(0, MOE_BLOCK, body, 0, unroll=8)

    def wait_rows(slot):
        for r in range(MOE_BLOCK):
            row_copy(0, slot, r).wait()

    n_blocks = pl.num_programs(0)
    slot = lax.rem(i, 2)
    nslot = 1 - slot
    nxt = jnp.minimum(i + 1, n_blocks - 1)

    @pl.when(i == 0)
    def _first():
        issue(0, 0)

    @pl.when(i < n_used)
    def _compute():
        e = be_ref[i]
        changed = (i == 0) | (e != be_ref[jnp.maximum(i - 1, 0)])

        @pl.when(changed)
        def _load_weights():
            copies = [pltpu.make_async_copy(w1_hbm.at[e], w1s, wsem.at[0]),
                      pltpu.make_async_copy(w3_hbm.at[e], w3s, wsem.at[1]),
                      pltpu.make_async_copy(w2_hbm.at[e], w2s, wsem.at[2])]
            for cp in copies:
                cp.start()
            for cp in copies:
                cp.wait()

        wait_rows(slot)
        xb = xbuf[slot].astype(BF16)
        acc = jnp.zeros(o_ref.shape, F32)
        d_ff = w1s.shape[1]
        n_chunks = d_ff // ffc
        rows_per_chunk = -(-MOE_BLOCK // n_chunks)
        for ci in range(n_chunks):
            c0 = ci * ffc
            a = jnp.dot(xb, w1s[:, c0:c0 + ffc], preferred_element_type=F32)
            g = jnp.dot(xb, w3s[:, c0:c0 + ffc], preferred_element_type=F32)
            acc = acc + jnp.dot((_silu(a) * g).astype(BF16), w2s[c0:c0 + ffc, :], preferred_element_type=F32)
            for r in range(ci * rows_per_chunk, min((ci + 1) * rows_per_chunk, MOE_BLOCK)):
                row_copy(st_ref[nxt * MOE_BLOCK + r], nslot, r).start()
        o_ref[...] = acc

    @pl.when(i >= n_used)
    def _unused():
        o_ref[...] = jnp.zeros_like(o_ref)
        wait_rows(slot)
        issue(nxt, nslot)

    @pl.when(i == n_blocks - 1)
    def _drain():
        wait_rows(nslot)


def _moe_mm(h_flat, slot_tok, block_e, n_used, w1, w3, w2):
    n, d = h_flat.shape
    n_blocks = block_e.shape[0]
    d_ff = w1.shape[2]
    any_spec = pl.BlockSpec(memory_space=pl.ANY)
    return pl.pallas_call(
        functools.partial(_moe_mm_kernel, ffc=512),
        grid_spec=pltpu.PrefetchScalarGridSpec(
            num_scalar_prefetch=3,
            grid=(n_blocks,),
            in_specs=[any_spec, any_spec, any_spec, any_spec],
            out_specs=pl.BlockSpec((MOE_BLOCK, d), lambda i, be, st, nu: (i, 0)),
            scratch_shapes=[pltpu.VMEM((2, MOE_BLOCK, d), F32),
                            pltpu.VMEM((d, d_ff), BF16), pltpu.VMEM((d, d_ff), BF16), pltpu.VMEM((d_ff, d), BF16),
                            pltpu.SemaphoreType.DMA((2,)), pltpu.SemaphoreType.DMA((3,))]),
        out_shape=jax.ShapeDtypeStruct((n_blocks * MOE_BLOCK, d), F32),
        compiler_params=_cparams(("arbitrary",), 56),
        name="moe_grouped_swiglu",
    )(block_e, slot_tok, n_used, h_flat, w1, w3, w2)


def _moe_comb_kernel(inv_ref, x_ref, mod_ref, slab_ref, fg_ref, yb_hbm, o_ref, gbuf, gsem, *, nj, final):
    b = pl.program_id(0)
    j = pl.program_id(1)
    t = b * nj + j
    nt = pl.num_programs(0) * nj

    def row_copy(src, slot, which, r):
        return pltpu.make_async_copy(yb_hbm.at[pl.ds(src, 1)], gbuf.at[slot, which, pl.ds(r, 1)], gsem.at[slot])

    def issue(tile, slot):
        def body(r, carry):
            base = (tile * TM + r) * 2
            row_copy(inv_ref[base], slot, 0, r).start()
            row_copy(inv_ref[base + 1], slot, 1, r).start()
            return carry
        lax.fori_loop(0, TM, body, 0, unroll=8)

    def wait_rows(slot):
        for r in range(TM):
            row_copy(0, slot, 0, r).wait()
            row_copy(0, slot, 1, r).wait()

    @pl.when(t == 0)
    def _first():
        issue(0, 0)

    @pl.when(t + 1 < nt)
    def _prefetch():
        issue(t + 1, lax.rem(t + 1, 2))

    slot = lax.rem(t, 2)
    wait_rows(slot)
    slab = slab_ref[0]
    y = slab[:, 2:3] * gbuf[slot, 0] + slab[:, 3:4] * gbuf[slot, 1]
    xn = x_ref[0] + mod_ref[5:6, :] * y
    if final:
        xn = _rms(xn) * fg_ref[...]
    o_ref[0] = xn


def _moe_combine(inv, xs, modtab, slab, final_g, yb, n_ctx_tiles, final):
    b, s, d = xs.shape
    nj = s // TM
    return pl.pallas_call(
        functools.partial(_moe_comb_kernel, nj=nj, final=final),
        grid_spec=pltpu.PrefetchScalarGridSpec(
            num_scalar_prefetch=1,
            grid=(b, nj),
            in_specs=[pl.BlockSpec((1, TM, d), lambda b, j, inv: (b, j, 0)),
                      pl.BlockSpec((None, None, 6, D_MODEL),
                                   lambda b, j, inv: (b, jnp.where(j >= n_ctx_tiles, 1, 0), 0, 0)),
                      pl.BlockSpec((1, TM, LANES), lambda b, j, inv: (b, j, 0)),
                      pl.BlockSpec((1, d), lambda b, j, inv: (0, 0)),
                      pl.BlockSpec(memory_space=pl.ANY)],
            out_specs=pl.BlockSpec((1, TM, d), lambda b, j, inv: (b, j, 0)),
            scratch_shapes=[pltpu.VMEM((2, 2, TM, d), F32), pltpu.SemaphoreType.DMA((2,))]),
        out_shape=jax.ShapeDtypeStruct((b, s, d), F32),
        compiler_params=_cparams(("arbitrary", "arbitrary")),
        name="moe_combine",
    )(inv, xs, modtab, slab, final_g.reshape(1, d), yb)


def _blockdiag(w):
    z = jnp.zeros_like(w[0])
    return jnp.concatenate([jnp.concatenate([w[0], z], axis=1), jnp.concatenate([z, w[1]], axis=1)], axis=0)


def kernel(x, c, ctx, c_ctx, ada_w, ada_b, norm_mix_g, norm_ffn_g, w_in, shift_mu, w0, w2, a0, a2, g2, k_k, k_a, r_k,
           ln_x_w, ln_x_b, na_rpb, w_out, ffn_w1, ffn_w3, ffn_w2, router, moe_w1, moe_w3, moe_w2, final_g):
    b, t, d = x.shape
    n_ctx = ctx.shape[1]
    s = n_ctx + t
    depth = ada_w.shape[0]
    assert d == D_MODEL and n_ctx % TM == 0 and t % TM == 0
    n_ctx_tiles = n_ctx // TM
    lanes = b * H_RWKV

    nb = -(-(b + 1) // 8) * 8
    c_all = jnp.concatenate([c, c_ctx[None, :], jnp.zeros((nb - b - 1, d), F32)], axis=0)
    mods = _mods(c_all, ada_w, ada_b).reshape(depth, nb, 6, d)
    modtab = jnp.stack([jnp.broadcast_to(mods[:, b:b + 1], (depth, b, 6, d)), mods[:, :b]], axis=2)

    head_id = np.arange(D_RWKV) // HEAD_DIM
    bd = jnp.asarray(head_id[:, None] == head_id[None, :], BF16)

    xs = jnp.concatenate([ctx, x], axis=1)
    for l in range(depth):
        last = l == depth - 1
        qkv, u_raw = _inproj(xs, modtab[l], norm_mix_g[l], w_in[l].astype(BF16), n_ctx_tiles)
        feat, rd = _features(u_raw, n_ctx, shift_mu[l], w0[l], _blockdiag(w2[l]), a0[l], _blockdiag(a2[l]), g2[l],
                             k_k[l], k_a[l], r_k[l], bd)
        y_na = _na(qkv, _na_bias_table(na_rpb[l]), n_ctx, rd)
        feat_t = feat.reshape(b, s, N_FEAT, H_RWKV, HEAD_DIM).transpose(1, 2, 4, 0, 3).reshape(
            s, N_FEAT, HEAD_DIM, lanes)
        yf_t, yb_t = _scan(feat_t, n_ctx, y_na)

        def untranspose(y):
            return y.reshape(s, HEAD_DIM, b, H_RWKV).transpose(2, 0, 3, 1).reshape(b, s, D_RWKV)

        xs = _outproj(untranspose(yf_t), untranspose(yb_t), rd, y_na, xs, modtab[l], ln_x_w[l], ln_x_b[l], bd,
                      w_out[l].astype(BF16), n_ctx_tiles)
        i = l // 2
        if l % 2 == 0:
            xs = _ffn(xs, modtab[l], norm_ffn_g[l], ffn_w1[i].astype(BF16), ffn_w3[i].astype(BF16),
                      ffn_w2[i].astype(BF16), n_ctx_tiles)
        else:
            xm, nct = (xs[:, n_ctx:], 0) if last else (xs, n_ctx_tiles)
            n_tok = xm.shape[0] * xm.shape[1]
            router_pad = jnp.pad(router[i], ((0, 0), (0, LANES - N_EXPERTS)))
            h, slab = _router(xm, modtab[l], norm_ffn_g[l], router_pad, nct)
            idx = slab[..., 0:2].astype(jnp.int32).reshape(n_tok, 2)
            slot_tok, block_e, n_used, dest = _dispatch(idx)
            yb = _moe_mm(h.reshape(n_tok, d), slot_tok, block_e, n_used, moe_w1[i].astype(BF16),
                         moe_w3[i].astype(BF16), moe_w2[i].astype(BF16))
            xs = _moe_combine(dest, xm, modtab[l], slab, final_g, yb, nct, final=last)
    if depth % 2 == 1:
        raise NotImplementedError("final norm is fused into the MoE combine of the last (odd) layer")
    return xs
```

```python
import functools

import numpy as np
import jax
import jax.numpy as jnp
from jax import lax
from jax.experimental import pallas as pl
from jax.experimental.pallas import tpu as pltpu

F32 = jnp.float32
BF16 = jnp.bfloat16
HI = lax.Precision.HIGHEST

D_MODEL = 1024
HEAD_DIM = 64
GRID_W = 64
D_RWKV = 512
D_NA = 512
H_RWKV = D_RWKV // HEAD_DIM
H_NA = D_NA // HEAD_DIM
R_DECAY = 64
R_AAA = 64
R_GATE = 128
D_SHIFT = 3 * D_RWKV + 2 * R_DECAY + 2 * R_AAA + R_GATE
N_QKV = 3 * D_NA
N_EXPERTS = 8
MOE_BLOCK = 256
RMS_EPS = 1e-6
GN_EPS = 64e-5
L2_EPS = 1e-12
NEG = -1e30

TM = 256
NA_WIN_ROWS = 12
N_FEAT = 9
TC_SCAN = 16
MOE_FF_CHUNK = 512
LANES = 128


def _cparams(sem, vmem_mb=48):
    return pltpu.CompilerParams(dimension_semantics=sem, vmem_limit_bytes=vmem_mb * 1024 * 1024)


def _sigmoid(x):
    return 1.0 / (1.0 + jnp.exp(-x))


def _silu(x):
    return x * _sigmoid(x)


def _softplus(x):
    return jnp.maximum(x, 0.0) + jnp.log(1.0 + jnp.exp(-jnp.abs(x)))


def _rms(x):
    return x * lax.rsqrt(jnp.mean(x * x, axis=-1, keepdims=True) + RMS_EPS)


def _norm_mod(x, g, scale, shift):
    return _rms(x) * g * (1.0 + scale) + shift


def _head_sum(x, bd):
    hi = x.astype(BF16)
    lo = (x - hi.astype(F32)).astype(BF16)
    return jnp.dot(hi, bd, preferred_element_type=F32) + jnp.dot(lo, bd, preferred_element_type=F32)


def _dot_nt(a, b):
    return lax.dot_general(a, b, (((1,), (1,)), ((), ())), preferred_element_type=F32)


def _mods_kernel(c_ref, w_ref, b_ref, o_ref):
    o_ref[0] = jnp.dot(_silu(c_ref[...]), w_ref[0], precision=HI, preferred_element_type=F32) + b_ref[0]


def _mods(c_all, ada_w, ada_b):
    depth, d, n6 = ada_w.shape
    nb = c_all.shape[0]
    tn = 1536
    return pl.pallas_call(
        _mods_kernel,
        grid=(depth, n6 // tn),
        in_specs=[pl.BlockSpec((nb, d), lambda l, n: (0, 0)),
                  pl.BlockSpec((1, d, tn), lambda l, n: (l, 0, n)),
                  pl.BlockSpec((1, 1, tn), lambda l, n: (l, 0, n))],
        out_specs=pl.BlockSpec((1, nb, tn), lambda l, n: (l, 0, n)),
        out_shape=jax.ShapeDtypeStruct((depth, nb, n6), F32),
        compiler_params=_cparams(("arbitrary", "arbitrary")),
        name="adaln_mods",
    )(c_all, ada_w, ada_b.reshape(depth, 1, n6))


def _mod_spec(n_ctx_tiles):
    return pl.BlockSpec((None, None, 6, D_MODEL), lambda b, j: (b, jnp.where(j >= n_ctx_tiles, 1, 0), 0, 0))


def _inproj_kernel(x_ref, mod_ref, g_ref, w_ref, qkv_ref, u_ref):
    h = _norm_mod(x_ref[0], g_ref[...], mod_ref[1:2, :], mod_ref[0:1, :]).astype(BF16)
    qkv_ref[0] = jnp.dot(h, w_ref[:, :N_QKV], preferred_element_type=F32).astype(BF16)
    u_ref[0] = jnp.dot(h, w_ref[:, N_QKV:], preferred_element_type=F32)


def _inproj(xs, modtab, g, w_in_bf, n_ctx_tiles):
    b, s, d = xs.shape
    nj = s // TM
    return pl.pallas_call(
        _inproj_kernel,
        grid=(b, nj),
        in_specs=[pl.BlockSpec((1, TM, d), lambda b, j: (b, j, 0)),
                  _mod_spec(n_ctx_tiles),
                  pl.BlockSpec((1, d), lambda b, j: (0, 0)),
                  pl.BlockSpec((d, N_QKV + D_SHIFT), lambda b, j: (0, 0))],
        out_specs=[pl.BlockSpec((1, TM, N_QKV), lambda b, j: (b, j, 0)),
                   pl.BlockSpec((1, TM, D_SHIFT), lambda b, j: (b, j, 0))],
        out_shape=[jax.ShapeDtypeStruct((b, s, N_QKV), BF16),
                   jax.ShapeDtypeStruct((b, s, D_SHIFT), F32)],
        compiler_params=_cparams(("arbitrary", "arbitrary")),
        name="in_proj",
    )(xs, modtab, g.reshape(1, d), w_in_bf)


def _na_bias_table(rpb):
    nh = rpb.shape[0]
    cq = np.arange(GRID_W)[:, None]
    ck = np.arange(GRID_W)[None, :]
    c0 = np.clip(cq - 8, 0, GRID_W - 16)
    col_ok = (ck >= c0) & (ck < c0 + 16)
    col_off = np.clip(ck - cq + 15, 0, 30)
    t = jnp.take(rpb, jnp.asarray(col_off.reshape(-1), jnp.int32), axis=2).reshape(nh, 15, GRID_W, GRID_W)
    t = jnp.where(col_ok[None, None], t, NEG)
    tabs = []
    for case in range(3):
        per_row = []
        for i in range(4):
            lo, ro = ((0, 7 - i), (i, 3), (4, 3 - i))[case]
            blk = jnp.pad(t[:, ro:ro + 8], ((0, 0), (lo, NA_WIN_ROWS - 8 - lo), (0, 0), (0, 0)),
                          constant_values=NEG)
            per_row.append(blk.transpose(0, 2, 1, 3))
        tabs.append(jnp.stack(per_row, axis=1).reshape(nh, TM, NA_WIN_ROWS * GRID_W))
    return jnp.stack(tabs, axis=1)


def _na_kernel(q_ref, k_ref, v_ref, bias_ref, after_ref, o_ref, *, n_ctx, rows):
    del after_ref
    j = pl.program_id(1)
    n_ctx_tiles = n_ctx // TM
    scale = HEAD_DIM ** -0.5
    nwin = NA_WIN_ROWS * GRID_W

    def head(h):
        return slice(h * HEAD_DIM, (h + 1) * HEAD_DIM)

    @pl.when(j < n_ctx_tiles)
    def _ctx():
        for h in range(H_NA):
            q = q_ref[0, :, head(h)]
            s = _dot_nt(q, k_ref[0, 0:n_ctx, head(h)]) * scale
            p = jnp.exp(s - jnp.max(s, axis=-1, keepdims=True))
            y = jnp.dot(p.astype(BF16), v_ref[0, 0:n_ctx, head(h)], preferred_element_type=F32)
            o_ref[0, :, head(h)] = (y / jnp.sum(p, axis=-1, keepdims=True)).astype(BF16)

    @pl.when(j >= n_ctx_tiles)
    def _lat():
        m = j - n_ctx_tiles
        w0 = jnp.clip(4 * m - 4, 0, rows - NA_WIN_ROWS)
        start = pl.multiple_of(n_ctx + w0 * GRID_W, GRID_W)
        for h in range(H_NA):
            q = q_ref[0, :, head(h)]
            s_loc = _dot_nt(q, k_ref[0, pl.ds(start, nwin), head(h)]) * scale + bias_ref[h]
            s_ctx = _dot_nt(q, k_ref[0, 0:n_ctx, head(h)]) * scale
            mx = jnp.maximum(jnp.max(s_loc, axis=-1, keepdims=True), jnp.max(s_ctx, axis=-1, keepdims=True))
            p_loc = jnp.exp(s_loc - mx)
            p_ctx = jnp.exp(s_ctx - mx)
            den = jnp.sum(p_loc, axis=-1, keepdims=True) + jnp.sum(p_ctx, axis=-1, keepdims=True)
            y = (jnp.dot(p_loc.astype(BF16), v_ref[0, pl.ds(start, nwin), head(h)], preferred_element_type=F32)
                 + jnp.dot(p_ctx.astype(BF16), v_ref[0, 0:n_ctx, head(h)], preferred_element_type=F32))
            o_ref[0, :, head(h)] = (y / den).astype(BF16)


def _na(qkv, bias_tab, n_ctx, after):
    b, s, _ = qkv.shape
    nj = s // TM
    n_ctx_tiles = n_ctx // TM
    rows = (s - n_ctx) // GRID_W
    n_lat_tiles = nj - n_ctx_tiles
    assert rows >= NA_WIN_ROWS and rows % 4 == 0 and n_lat_tiles >= 3

    def case_map(b, j):
        m = j - n_ctx_tiles
        return (0, jnp.where(m <= 0, 0, jnp.where(m >= n_lat_tiles - 1, 2, 1)), 0, 0)

    return pl.pallas_call(
        functools.partial(_na_kernel, n_ctx=n_ctx, rows=rows),
        grid=(b, nj),
        in_specs=[pl.BlockSpec((1, TM, D_NA), lambda b, j: (b, j, 0)),
                  pl.BlockSpec((1, s, D_NA), lambda b, j: (b, 0, 1)),
                  pl.BlockSpec((1, s, D_NA), lambda b, j: (b, 0, 2)),
                  pl.BlockSpec((H_NA, None, TM, NA_WIN_ROWS * GRID_W), case_map),
                  pl.BlockSpec((1, 8, LANES), lambda b, j: (0, 0, 0))],
        out_specs=pl.BlockSpec((1, TM, D_NA), lambda b, j: (b, j, 0)),
        out_shape=jax.ShapeDtypeStruct((b, s, D_NA), BF16),
        compiler_params=_cparams(("arbitrary", "arbitrary")),
        name="na_attention",
    )(qkv, qkv, qkv, bias_tab, after)


def _feat_kernel(u_ref, up_ref, un_ref, mu_ref, w0_ref, w2_ref, a0_ref, a2_ref, g2_ref, kk_ref, ka_ref, rk_ref,
                 bd_ref, feat_ref, rd_ref, *, n_ctx_tiles, nj):
    j = pl.program_id(1)
    seg_start = (j == 0) | (j == n_ctx_tiles)
    seg_end = (j == n_ctx_tiles - 1) | (j == nj - 1)
    rid = lax.broadcasted_iota(jnp.int32, (TM, 1), 0)

    def shifted(lo, hi):
        p = u_ref[0, :, lo:hi]
        prev_row = jnp.where(seg_start, 0.0, up_ref[0, 7:8, lo:hi])
        next_row = jnp.where(seg_end, 0.0, un_ref[0, 0:1, lo:hi])
        prev = jnp.where(rid == 0, prev_row, pltpu.roll(p, 1, axis=0))
        nxt = jnp.where(rid == TM - 1, next_row, pltpu.roll(p, TM - 1, axis=0))
        return p + mu_ref[:, lo:hi] * (0.5 * (prev + nxt) - p)

    c = D_RWKV
    r = shifted(0, c)
    k = shifted(c, 2 * c)
    v = shifted(2 * c, 3 * c)
    o = 3 * c
    wl = shifted(o, o + 2 * R_DECAY)
    al = shifted(o + 2 * R_DECAY, o + 2 * R_DECAY + 2 * R_AAA)
    gl = shifted(o + 2 * R_DECAY + 2 * R_AAA, D_SHIFT)
    bd = bd_ref[...]

    lw = jnp.dot(jnp.tanh(wl), w2_ref[...], precision=HI, preferred_element_type=F32)
    la = jnp.dot(al, a2_ref[...], precision=HI, preferred_element_type=F32)
    g = jnp.dot(_sigmoid(gl), g2_ref[...], precision=HI, preferred_element_type=F32)

    kks = k * kk_ref[...]
    kk = kks / jnp.maximum(jnp.sqrt(_head_sum(kks * kks, bd)), L2_EPS)
    feat_ref[0, :, 0:c] = r
    feat_ref[0, :, c:2 * c] = v
    feat_ref[0, :, 2 * c:3 * c] = kk
    ksum = jnp.zeros_like(k)
    for z in range(2):
        w_log = -_softplus(-(w0_ref[z:z + 1, :] + lw[:, z * c:(z + 1) * c])) - 0.5
        decay = jnp.exp(-jnp.exp(w_log))
        a = _sigmoid(a0_ref[z:z + 1, :] + la[:, z * c:(z + 1) * c])
        k_dir = k * (1.0 + (a - 1.0) * ka_ref[...])
        ksum = ksum + k_dir
        base = (3 + 3 * z) * c
        feat_ref[0, :, base:base + c] = decay
        feat_ref[0, :, base + c:base + 2 * c] = k_dir
        feat_ref[0, :, base + 2 * c:base + 3 * c] = kk * a
    bonus = _head_sum(r * (0.5 * ksum) * rk_ref[...], bd) * v
    rd_ref[0, :, 0:c] = bonus
    rd_ref[0, :, c:2 * c] = g


def _features(u_raw, n_ctx, shift_mu, w0, w2blk, a0, a2blk, g2, k_k, k_a, r_k, bd):
    b, s, _ = u_raw.shape
    nj = s // TM
    n_ctx_tiles = n_ctx // TM
    c = D_RWKV
    t8 = TM // 8

    def full(shape):
        return pl.BlockSpec(shape, lambda b, j: (0,) * len(shape))

    return pl.pallas_call(
        functools.partial(_feat_kernel, n_ctx_tiles=n_ctx_tiles, nj=nj),
        grid=(b, nj),
        in_specs=[pl.BlockSpec((1, TM, D_SHIFT), lambda b, j: (b, j, 0)),
                  pl.BlockSpec((1, 8, D_SHIFT), lambda b, j: (b, jnp.maximum(j * t8 - 1, 0), 0)),
                  pl.BlockSpec((1, 8, D_SHIFT), lambda b, j: (b, jnp.minimum((j + 1) * t8, s // 8 - 1), 0)),
                  full((1, D_SHIFT)), full((2, c)), full((2 * R_DECAY, 2 * c)), full((2, c)),
                  full((2 * R_AAA, 2 * c)), full((R_GATE, c)), full((1, c)), full((1, c)), full((1, c)),
                  full((c, c))],
        out_specs=[pl.BlockSpec((1, TM, N_FEAT * c), lambda b, j: (b, j, 0)),
                   pl.BlockSpec((1, TM, 2 * c), lambda b, j: (b, j, 0))],
        out_shape=[jax.ShapeDtypeStruct((b, s, N_FEAT * c), F32),
                   jax.ShapeDtypeStruct((b, s, 2 * c), F32)],
        compiler_params=_cparams(("arbitrary", "arbitrary")),
        name="rwkv_features",
    )(u_raw, u_raw, u_raw, shift_mu.reshape(1, D_SHIFT), w0, w2blk, a0, a2blk, g2, k_k.reshape(1, c),
      k_a.reshape(1, c), r_k.reshape(1, c), bd)


def _scan_kernel(fs_ref, fd_ref, bs_ref, bd_ref, after_ref, yf_ref, yb_ref, st_ref, p_ref, op_ref):
    del after_ref

    @pl.when(pl.program_id(0) == 0)
    def _init():
        st_ref[...] = jnp.zeros_like(st_ref)

    kg = 8
    n_groups = HEAD_DIM // kg
    p_ref[...] = jnp.ones_like(p_ref)

    def tile(ref, row, a):
        return ref[row, a].reshape(HEAD_DIM, -1)

    def one_step(d, s_ref, d_ref, y_ref, row):
        v_t = tile(s_ref, row, 1)
        p_prev = p_ref[d]
        p_new = p_prev * tile(d_ref, row, 0)
        inv = 1.0 / p_new
        p_ref[d] = p_new
        op_ref[d, 0] = (tile(s_ref, row, 2) * p_prev).reshape(n_groups, kg, -1)
        op_ref[d, 1] = (tile(d_ref, row, 2) * inv).reshape(n_groups, kg, -1)
        op_ref[d, 2] = (tile(d_ref, row, 1) * inv).reshape(n_groups, kg, -1)
        op_ref[d, 3] = (tile(s_ref, row, 0) * p_new).reshape(n_groups, kg, -1)

        sa = jnp.zeros_like(v_t)
        for k in range(HEAD_DIM):
            sa = sa + st_ref[d, k] * op_ref[d, 0, k // kg, pl.ds(k % kg, 1), :]

        def pass_update(g, y):
            for kk in range(kg):
                k = g * kg + kk
                z_new = st_ref[d, k] - sa * op_ref[d, 1, g, pl.ds(kk, 1), :] + v_t * op_ref[d, 2, g, pl.ds(kk, 1), :]
                st_ref[d, k] = z_new
                y = y + z_new * op_ref[d, 3, g, pl.ds(kk, 1), :]
            return y

        y_ref[row] = lax.fori_loop(0, n_groups, pass_update, jnp.zeros_like(v_t))

    def step(jj, carry):
        one_step(0, fs_ref, fd_ref, yf_ref, jj)
        one_step(1, bs_ref, bd_ref, yb_ref, TC_SCAN - 1 - jj)
        return carry

    lax.fori_loop(0, TC_SCAN, step, 0)

    for d in range(2):
        for k in range(HEAD_DIM):
            st_ref[d, k] = st_ref[d, k] * p_ref[d, pl.ds(k, 1), :]


def _scan(feat_t, n_ctx, after):
    s, _, hd, lanes = feat_t.shape
    feat_t = feat_t.reshape(s, N_FEAT, hd // 8, 8, lanes)
    nblk = s // TC_SCAN
    nc = n_ctx // TC_SCAN
    assert n_ctx % TC_SCAN == 0 and s % TC_SCAN == 0

    def bwd(i):
        return jnp.where(i < nc, nc - 1 - i, nblk - 1 - i + nc)

    blk = (TC_SCAN, 3, hd // 8, 8, lanes)
    yblk = (TC_SCAN, hd, lanes)
    return pl.pallas_call(
        _scan_kernel,
        grid=(nblk,),
        in_specs=[pl.BlockSpec(blk, lambda i: (i, 0, 0, 0, 0)),
                  pl.BlockSpec(blk, lambda i: (i, 1, 0, 0, 0)),
                  pl.BlockSpec(blk, lambda i: (bwd(i), 0, 0, 0, 0)),
                  pl.BlockSpec(blk, lambda i: (bwd(i), 2, 0, 0, 0)),
                  pl.BlockSpec((1, 16, after.shape[2]), lambda i: (0, 0, 0))],
        out_specs=[pl.BlockSpec(yblk, lambda i: (i, 0, 0)),
                   pl.BlockSpec(yblk, lambda i: (bwd(i), 0, 0))],
        out_shape=[jax.ShapeDtypeStruct((s, hd, lanes), F32)] * 2,
        scratch_shapes=[pltpu.VMEM((2, HEAD_DIM, hd, lanes), F32),
                        pltpu.VMEM((2, hd, lanes), F32),
                        pltpu.VMEM((2, 4, hd // 8, 8, lanes), F32)],
        compiler_params=_cparams(("arbitrary",)),
        name="rwkv_scan",
    )(feat_t, feat_t, feat_t, feat_t, after)


def _outproj_kernel(yf_ref, yb_ref, rd_ref, yna_ref, x_ref, mod_ref, lnw_ref, lnb_ref, bd_ref, w_ref, o_ref):
    c = D_RWKV
    bd = bd_ref[...]
    y = yf_ref[0] + yb_ref[0]
    mu = _head_sum(y, bd) * (1.0 / HEAD_DIM)
    dlt = y - mu
    var = _head_sum(dlt * dlt, bd) * (1.0 / HEAD_DIM)
    yn = dlt * lax.rsqrt(var + GN_EPS) * lnw_ref[...] + lnb_ref[...]
    y_rw = ((yn + rd_ref[0, :, 0:c]) * rd_ref[0, :, c:2 * c]).astype(BF16)
    o = (jnp.dot(y_rw, w_ref[0:c, :], preferred_element_type=F32)
         + jnp.dot(yna_ref[0], w_ref[c:, :], preferred_element_type=F32))
    o_ref[0] = x_ref[0] + mod_ref[2:3, :] * o


def _outproj(yf, yb, rd, yna, xs, modtab, ln_w, ln_b, bd, w_out_bf, n_ctx_tiles):
    b, s, d = xs.shape
    nj = s // TM
    c = D_RWKV

    def tile(w):
        return pl.BlockSpec((1, TM, w), lambda b, j: (b, j, 0))

    def full(shape):
        return pl.BlockSpec(shape, lambda b, j: (0,) * len(shape))

    return pl.pallas_call(
        _outproj_kernel,
        grid=(b, nj),
        in_specs=[tile(c), tile(c), tile(2 * c), tile(D_NA), tile(d), _mod_spec(n_ctx_tiles),
                  full((1, c)), full((1, c)), full((c, c)), full((d, d))],
        out_specs=tile(d),
        out_shape=jax.ShapeDtypeStruct((b, s, d), F32),
        compiler_params=_cparams(("arbitrary", "arbitrary")),
        name="out_proj",
    )(yf, yb, rd, yna, xs, modtab, ln_w.reshape(1, c), ln_b.reshape(1, c), bd, w_out_bf)


def _ffn_kernel(x_ref, mod_ref, g_ref, w1_ref, w3_ref, w2_ref, o_ref, *, ffc):
    x = x_ref[0]
    h = _norm_mod(x, g_ref[...], mod_ref[4:5, :], mod_ref[3:4, :]).astype(BF16)
    d_ff = w1_ref.shape[1]
    acc = jnp.zeros(x.shape, F32)
    for c0 in range(0, d_ff, ffc):
        a = jnp.dot(h, w1_ref[:, c0:c0 + ffc], preferred_element_type=F32)
        g = jnp.dot(h, w3_ref[:, c0:c0 + ffc], preferred_element_type=F32)
        acc = acc + jnp.dot((_silu(a) * g).astype(BF16), w2_ref[c0:c0 + ffc, :], preferred_element_type=F32)
    o_ref[0] = x + mod_ref[5:6, :] * acc


def _ffn(xs, modtab, g, w1, w3, w2, n_ctx_tiles):
    b, s, d = xs.shape
    nj = s // TM
    d_ff = w1.shape[1]
    const = lambda b, j: (0, 0)
    return pl.pallas_call(
        functools.partial(_ffn_kernel, ffc=d_ff // 2),
        grid=(b, nj),
        in_specs=[pl.BlockSpec((1, TM, d), lambda b, j: (b, j, 0)), _mod_spec(n_ctx_tiles),
                  pl.BlockSpec((1, d), const),
                  pl.BlockSpec((d, d_ff), const), pl.BlockSpec((d, d_ff), const), pl.BlockSpec((d_ff, d), const)],
        out_specs=pl.BlockSpec((1, TM, d), lambda b, j: (b, j, 0)),
        out_shape=jax.ShapeDtypeStruct((b, s, d), F32),
        compiler_params=_cparams(("arbitrary", "arbitrary"), 56),
        name="ffn_swiglu",
    )(xs, modtab, g.reshape(1, d), w1, w3, w2)


def _router_kernel(x_ref, mod_ref, g_ref, r_ref, h_ref, slab_ref):
    h = _norm_mod(x_ref[0], g_ref[...], mod_ref[4:5, :], mod_ref[3:4, :])
    h_ref[0] = h
    logits = jnp.dot(h, r_ref[...], precision=HI, preferred_element_type=F32)
    lane = lax.broadcasted_iota(jnp.int32, logits.shape, 1)
    lg = jnp.where(lane < N_EXPERTS, logits, -jnp.inf)
    m1 = jnp.max(lg, axis=-1, keepdims=True)
    i1 = jnp.min(jnp.where(lg == m1, lane, LANES), axis=-1, keepdims=True)
    lg2 = jnp.where(lane == i1, -jnp.inf, lg)
    m2 = jnp.max(lg2, axis=-1, keepdims=True)
    i2 = jnp.min(jnp.where(lg2 == m2, lane, LANES), axis=-1, keepdims=True)
    e = jnp.exp(m2 - m1)
    g1 = 1.0 / (1.0 + e)
    g2 = e / (1.0 + e)
    slab_ref[0] = jnp.where(lane == 0, i1.astype(F32),
                            jnp.where(lane == 1, i2.astype(F32),
                                      jnp.where(lane == 2, g1, jnp.where(lane == 3, g2, 0.0))))


def _router(xs, modtab, g, router_pad, n_ctx_tiles):
    b, s, d = xs.shape
    nj = s // TM
    const = lambda b, j: (0, 0)
    return pl.pallas_call(
        _router_kernel,
        grid=(b, nj),
        in_specs=[pl.BlockSpec((1, TM, d), lambda b, j: (b, j, 0)), _mod_spec(n_ctx_tiles),
                  pl.BlockSpec((1, d), const), pl.BlockSpec((d, LANES), const)],
        out_specs=[pl.BlockSpec((1, TM, d), lambda b, j: (b, j, 0)),
                   pl.BlockSpec((1, TM, LANES), lambda b, j: (b, j, 0))],
        out_shape=[jax.ShapeDtypeStruct((b, s, d), F32), jax.ShapeDtypeStruct((b, s, LANES), F32)],
        compiler_params=_cparams(("arbitrary", "arbitrary")),
        name="moe_router",
    )(xs, modtab, g.reshape(1, d), router_pad)


def _dispatch(idx):
    n = idx.shape[0]
    a = n * 2
    flat_e = idx.reshape(a)
    onehot = (flat_e[None, :] == jnp.arange(N_EXPERTS, dtype=jnp.int32)[:, None]).astype(jnp.int32)
    csum = jnp.cumsum(onehot, axis=1)
    counts = csum[:, -1]
    rank = jnp.sum(csum * onehot, axis=0) - 1
    padded = (counts + MOE_BLOCK - 1) // MOE_BLOCK * MOE_BLOCK
    pad_ends = jnp.cumsum(padded)
    pad_starts = pad_ends - padded
    dest = (pad_starts[flat_e] + rank).astype(jnp.int32)
    n_blocks = -(-a // MOE_BLOCK) + N_EXPERTS
    slot_tok = jnp.zeros((n_blocks * MOE_BLOCK,), jnp.int32).at[dest].set(jnp.arange(a, dtype=jnp.int32) // 2)
    block_start = jnp.arange(n_blocks, dtype=jnp.int32) * MOE_BLOCK
    block_e = jnp.minimum(jnp.sum((pad_ends[None, :] <= block_start[:, None]).astype(jnp.int32), axis=1),
                          N_EXPERTS - 1)
    n_used = (pad_ends[-1] // MOE_BLOCK).astype(jnp.int32).reshape(1)
    return slot_tok, block_e, n_used, dest


def _moe_mm_kernel(be_ref, st_ref, nu_ref, h_hbm, w1_hbm, w3_hbm, w2_hbm, o_ref, xbuf, w1s, w3s, w2s, stage_cols,
                   stage_rows, gsem, wsem,
                   *, ffc):
    i = pl.program_id(0)
    n_used = nu_ref[0]

    def row_copy(tok, slot, r):
        return pltpu.make_async_copy(h_hbm.at[pl.ds(tok, 1)], xbuf.at[slot, pl.ds(r, 1)], gsem.at[slot])

    def issue(blk, slot):
        def body(r, carry):
            row_copy(st_ref[blk * MOE_BLOCK + r], slot, r).start()
            return carry
        lax.fori_loop(0, MOE_BLOCK, body, 0, unroll=8)

    def wait_rows(slot):
        for r in range(MOE_BLOCK):
            row_copy(0, slot, r).wait()

    n_blocks = pl.num_programs(0)
    slot = lax.rem(i, 2)
    nslot = 1 - slot
    nxt = jnp.minimum(i + 1, n_blocks - 1)

    @pl.when(i == 0)
    def _first():
        issue(0, 0)

    @pl.when(i < n_used)
    def _compute():
        e = be_ref[i]
        changed = (i == 0) | (e != be_ref[jnp.maximum(i - 1, 0)])

        @pl.when(changed)
        def _load_weights():
            nch = w1s.shape[1] // ffc
            chunks = [(m, c) for m in range(3) for c in range(nch)]

            def chunk_copy(n):
                m, c = chunks[n]
                if m < 2:
                    src = (w1_hbm, w3_hbm)[m].at[e, :, pl.ds(c * ffc, ffc)]
                    return pltpu.make_async_copy(src, stage_cols.at[n % 2], wsem.at[n % 2])
                return pltpu.make_async_copy(w2_hbm.at[e, pl.ds(c * ffc, ffc), :], stage_rows.at[n % 2],
                                             wsem.at[n % 2])

            chunk_copy(0).start()
            for n, (m, c) in enumerate(chunks):
                if n + 1 < len(chunks):
                    chunk_copy(n + 1).start()
                chunk_copy(n).wait()
                if m < 2:
                    (w1s, w3s)[m][:, c * ffc:(c + 1) * ffc] = stage_cols[n % 2].astype(BF16)
                else:
                    w2s[c * ffc:(c + 1) * ffc, :] = stage_rows[n % 2].astype(BF16)

        wait_rows(slot)
        xb = xbuf[slot].astype(BF16)
        acc = jnp.zeros(o_ref.shape, F32)
        d_ff = w1s.shape[1]
        n_chunks = d_ff // ffc
        rows_per_chunk = -(-MOE_BLOCK // n_chunks)
        for ci in range(n_chunks):
            c0 = ci * ffc
            a = jnp.dot(xb, w1s[:, c0:c0 + ffc], preferred_element_type=F32)
            g = jnp.dot(xb, w3s[:, c0:c0 + ffc], preferred_element_type=F32)
            acc = acc + jnp.dot((_silu(a) * g).astype(BF16), w2s[c0:c0 + ffc, :], preferred_element_type=F32)
            for r in range(ci * rows_per_chunk, min((ci + 1) * rows_per_chunk, MOE_BLOCK)):
                row_copy(st_ref[nxt * MOE_BLOCK + r], nslot, r).start()
        o_ref[...] = acc

    @pl.when(i >= n_used)
    def _unused():
        o_ref[...] = jnp.zeros_like(o_ref)
        wait_rows(slot)
        issue(nxt, nslot)

    @pl.when(i == n_blocks - 1)
    def _drain():
        wait_rows(nslot)


def _moe_mm(h_flat, slot_tok, block_e, n_used, w1, w3, w2):
    n, d = h_flat.shape
    n_blocks = block_e.shape[0]
    d_ff = w1.shape[2]
    any_spec = pl.BlockSpec(memory_space=pl.ANY)
    return pl.pallas_call(
        functools.partial(_moe_mm_kernel, ffc=MOE_FF_CHUNK),
        grid_spec=pltpu.PrefetchScalarGridSpec(
            num_scalar_prefetch=3,
            grid=(n_blocks,),
            in_specs=[any_spec, any_spec, any_spec, any_spec],
            out_specs=pl.BlockSpec((MOE_BLOCK, d), lambda i, be, st, nu: (i, 0)),
            scratch_shapes=[pltpu.VMEM((2, MOE_BLOCK, d), F32),
                            pltpu.VMEM((d, d_ff), BF16), pltpu.VMEM((d, d_ff), BF16), pltpu.VMEM((d_ff, d), BF16),
                            pltpu.VMEM((2, d, MOE_FF_CHUNK), F32), pltpu.VMEM((2, MOE_FF_CHUNK, d), F32),
                            pltpu.SemaphoreType.DMA((2,)), pltpu.SemaphoreType.DMA((2,))]),
        out_shape=jax.ShapeDtypeStruct((n_blocks * MOE_BLOCK, d), F32),
        compiler_params=_cparams(("arbitrary",), 56),
        name="moe_grouped_swiglu",
    )(block_e, slot_tok, n_used, h_flat, w1, w3, w2)


def _moe_comb_kernel(inv_ref, x_ref, mod_ref, slab_ref, fg_ref, yb_hbm, o_ref, gbuf, gsem, *, nj, final):
    b = pl.program_id(0)
    j = pl.program_id(1)
    t = b * nj + j
    nt = pl.num_programs(0) * nj

    def row_copy(src, slot, which, r):
        return pltpu.make_async_copy(yb_hbm.at[pl.ds(src, 1)], gbuf.at[slot, which, pl.ds(r, 1)], gsem.at[slot])

    def issue(tile, slot):
        def body(r, carry):
            base = (tile * TM + r) * 2
            row_copy(inv_ref[base], slot, 0, r).start()
            row_copy(inv_ref[base + 1], slot, 1, r).start()
            return carry
        lax.fori_loop(0, TM, body, 0, unroll=8)

    def wait_rows(slot):
        for r in range(TM):
            row_copy(0, slot, 0, r).wait()
            row_copy(0, slot, 1, r).wait()

    @pl.when(t == 0)
    def _first():
        issue(0, 0)

    @pl.when(t + 1 < nt)
    def _prefetch():
        issue(t + 1, lax.rem(t + 1, 2))

    slot = lax.rem(t, 2)
    wait_rows(slot)
    slab = slab_ref[0]
    y = slab[:, 2:3] * gbuf[slot, 0] + slab[:, 3:4] * gbuf[slot, 1]
    xn = x_ref[0] + mod_ref[5:6, :] * y
    if final:
        xn = _rms(xn) * fg_ref[...]
    o_ref[0] = xn


def _moe_combine(inv, xs, modtab, slab, final_g, yb, n_ctx_tiles, final):
    b, s, d = xs.shape
    nj = s // TM
    return pl.pallas_call(
        functools.partial(_moe_comb_kernel, nj=nj, final=final),
        grid_spec=pltpu.PrefetchScalarGridSpec(
            num_scalar_prefetch=1,
            grid=(b, nj),
            in_specs=[pl.BlockSpec((1, TM, d), lambda b, j, inv: (b, j, 0)),
                      pl.BlockSpec((None, None, 6, D_MODEL),
                                   lambda b, j, inv: (b, jnp.where(j >= n_ctx_tiles, 1, 0), 0, 0)),
                      pl.BlockSpec((1, TM, LANES), lambda b, j, inv: (b, j, 0)),
                      pl.BlockSpec((1, d), lambda b, j, inv: (0, 0)),
                      pl.BlockSpec(memory_space=pl.ANY)],
            out_specs=pl.BlockSpec((1, TM, d), lambda b, j, inv: (b, j, 0)),
            scratch_shapes=[pltpu.VMEM((2, 2, TM, d), F32), pltpu.SemaphoreType.DMA((2,))]),
        out_shape=jax.ShapeDtypeStruct((b, s, d), F32),
        compiler_params=_cparams(("arbitrary", "arbitrary")),
        name="moe_combine",
    )(inv, xs, modtab, slab, final_g.reshape(1, d), yb)


def _blockdiag(w):
    z = jnp.zeros_like(w[0])
    return jnp.concatenate([jnp.concatenate([w[0], z], axis=1), jnp.concatenate([z, w[1]], axis=1)], axis=0)


def kernel(x, c, ctx, c_ctx, ada_w, ada_b, norm_mix_g, norm_ffn_g, w_in, shift_mu, w0, w2, a0, a2, g2, k_k, k_a, r_k,
           ln_x_w, ln_x_b, na_rpb, w_out, ffn_w1, ffn_w3, ffn_w2, router, moe_w1, moe_w3, moe_w2, final_g):
    b, t, d = x.shape
    n_ctx = ctx.shape[1]
    s = n_ctx + t
    depth = ada_w.shape[0]
    assert d == D_MODEL and n_ctx % TM == 0 and t % TM == 0
    n_ctx_tiles = n_ctx // TM
    lanes = b * H_RWKV

    nb = -(-(b + 1) // 8) * 8
    c_all = jnp.concatenate([c, c_ctx[None, :], jnp.zeros((nb - b - 1, d), F32)], axis=0)
    mods = _mods(c_all, ada_w, ada_b).reshape(depth, nb, 6, d)
    modtab = jnp.stack([jnp.broadcast_to(mods[:, b:b + 1], (depth, b, 6, d)), mods[:, :b]], axis=2)

    head_id = np.arange(D_RWKV) // HEAD_DIM
    bd = jnp.asarray(head_id[:, None] == head_id[None, :], BF16)

    xs = jnp.concatenate([ctx, x], axis=1)
    for l in range(depth):
        last = l == depth - 1
        qkv, u_raw = _inproj(xs, modtab[l], norm_mix_g[l], w_in[l].astype(BF16), n_ctx_tiles)
        feat, rd = _features(u_raw, n_ctx, shift_mu[l], w0[l], _blockdiag(w2[l]), a0[l], _blockdiag(a2[l]), g2[l],
                             k_k[l], k_a[l], r_k[l], bd)
        y_na = _na(qkv, _na_bias_table(na_rpb[l]), n_ctx, rd)
        feat_t = feat.reshape(b, s, N_FEAT, H_RWKV, HEAD_DIM).transpose(1, 2, 4, 0, 3).reshape(
            s, N_FEAT, HEAD_DIM, lanes)
        yf_t, yb_t = _scan(feat_t, n_ctx, y_na)

        def untranspose(y):
            return y.reshape(s, HEAD_DIM, b, H_RWKV).transpose(2, 0, 3, 1).reshape(b, s, D_RWKV)

        xs = _outproj(untranspose(yf_t), untranspose(yb_t), rd, y_na, xs, modtab[l], ln_x_w[l], ln_x_b[l], bd,
                      w_out[l].astype(BF16), n_ctx_tiles)
        i = l // 2
        if l % 2 == 0:
            xs = _ffn(xs, modtab[l], norm_ffn_g[l], ffn_w1[i].astype(BF16), ffn_w3[i].astype(BF16),
                      ffn_w2[i].astype(BF16), n_ctx_tiles)
        else:
            xm, nct = (xs[:, n_ctx:], 0) if last else (xs, n_ctx_tiles)
            n_tok = xm.shape[0] * xm.shape[1]
            router_pad = jnp.pad(router[i], ((0, 0), (0, LANES - N_EXPERTS)))
            h, slab = _router(xm, modtab[l], norm_ffn_g[l], router_pad, nct)
            idx = slab[..., 0:2].astype(jnp.int32).reshape(n_tok, 2)
            slot_tok, block_e, n_used, dest = _dispatch(idx)
            yb = _moe_mm(h.reshape(n_tok, d), slot_tok, block_e, n_used, moe_w1[i], moe_w3[i], moe_w2[i])
            xs = _moe_combine(dest, xm, modtab[l], slab, final_g, yb, nct, final=last)
    if depth % 2 == 1:
        raise NotImplementedError("final norm is fused into the MoE combine of the last (odd) layer")
    return xs
```

```python
import functools

import numpy as np
import jax
import jax.numpy as jnp
from jax import lax
from jax.experimental import pallas as pl
from jax.experimental.pallas import tpu as pltpu

F32 = jnp.float32
BF16 = jnp.bfloat16
HI = lax.Precision.HIGHEST

D_MODEL = 1024
HEAD_DIM = 64
GRID_W = 64
D_RWKV = 512
D_NA = 512
H_RWKV = D_RWKV // HEAD_DIM
H_NA = D_NA // HEAD_DIM
R_DECAY = 64
R_AAA = 64
R_GATE = 128
D_SHIFT = 3 * D_RWKV + 2 * R_DECAY + 2 * R_AAA + R_GATE
N_QKV = 3 * D_NA
N_EXPERTS = 8
MOE_BLOCK = 256
RMS_EPS = 1e-6
GN_EPS = 64e-5
L2_EPS = 1e-12
NEG = -1e30

TM = 256
NA_WIN_ROWS = 12
N_FEAT = 9
TC_SCAN = 16
MOE_FF_CHUNK = 512
LANES = 128


def _cparams(sem, vmem_mb=48):
    return pltpu.CompilerParams(dimension_semantics=sem, vmem_limit_bytes=vmem_mb * 1024 * 1024)


def _sigmoid(x):
    return 1.0 / (1.0 + jnp.exp(-x))


def _silu(x):
    return x * _sigmoid(x)


def _softplus(x):
    return jnp.maximum(x, 0.0) + jnp.log(1.0 + jnp.exp(-jnp.abs(x)))


def _rms(x):
    return x * lax.rsqrt(jnp.mean(x * x, axis=-1, keepdims=True) + RMS_EPS)


def _norm_mod(x, g, scale, shift):
    return _rms(x) * g * (1.0 + scale) + shift


def _head_sum(x, bd):
    hi = x.astype(BF16)
    lo = (x - hi.astype(F32)).astype(BF16)
    return jnp.dot(hi, bd, preferred_element_type=F32) + jnp.dot(lo, bd, preferred_element_type=F32)


def _dot_nt(a, b):
    return lax.dot_general(a, b, (((1,), (1,)), ((), ())), preferred_element_type=F32)


def _mods_kernel(c_ref, w_ref, b_ref, o_ref):
    o_ref[0] = jnp.dot(_silu(c_ref[...]), w_ref[0], precision=HI, preferred_element_type=F32) + b_ref[0]


def _mods(c_all, ada_w, ada_b):
    depth, d, n6 = ada_w.shape
    nb = c_all.shape[0]
    tn = 1536
    return pl.pallas_call(
        _mods_kernel,
        grid=(depth, n6 // tn),
        in_specs=[pl.BlockSpec((nb, d), lambda l, n: (0, 0)),
                  pl.BlockSpec((1, d, tn), lambda l, n: (l, 0, n)),
                  pl.BlockSpec((1, 1, tn), lambda l, n: (l, 0, n))],
        out_specs=pl.BlockSpec((1, nb, tn), lambda l, n: (l, 0, n)),
        out_shape=jax.ShapeDtypeStruct((depth, nb, n6), F32),
        compiler_params=_cparams(("arbitrary", "arbitrary")),
        name="adaln_mods",
    )(c_all, ada_w, ada_b.reshape(depth, 1, n6))


def _mod_spec(n_ctx_tiles):
    return pl.BlockSpec((None, None, 6, D_MODEL), lambda b, j: (b, jnp.where(j >= n_ctx_tiles, 1, 0), 0, 0))


def _inproj_kernel(x_ref, mod_ref, g_ref, w_ref, qkv_ref, u_ref):
    h = _norm_mod(x_ref[0], g_ref[...], mod_ref[1:2, :], mod_ref[0:1, :]).astype(BF16)
    qkv_ref[0] = jnp.dot(h, w_ref[:, :N_QKV], preferred_element_type=F32).astype(BF16)
    u_ref[0] = jnp.dot(h, w_ref[:, N_QKV:], preferred_element_type=F32)


def _inproj(xs, modtab, g, w_in_bf, n_ctx_tiles):
    b, s, d = xs.shape
    nj = s // TM
    return pl.pallas_call(
        _inproj_kernel,
        grid=(b, nj),
        in_specs=[pl.BlockSpec((1, TM, d), lambda b, j: (b, j, 0)),
                  _mod_spec(n_ctx_tiles),
                  pl.BlockSpec((1, d), lambda b, j: (0, 0)),
                  pl.BlockSpec((d, N_QKV + D_SHIFT), lambda b, j: (0, 0))],
        out_specs=[pl.BlockSpec((1, TM, N_QKV), lambda b, j: (b, j, 0)),
                   pl.BlockSpec((1, TM, D_SHIFT), lambda b, j: (b, j, 0))],
        out_shape=[jax.ShapeDtypeStruct((b, s, N_QKV), BF16),
                   jax.ShapeDtypeStruct((b, s, D_SHIFT), F32)],
        compiler_params=_cparams(("arbitrary", "arbitrary")),
        name="in_proj",
    )(xs, modtab, g.reshape(1, d), w_in_bf)


def _na_bias_table(rpb):
    nh = rpb.shape[0]
    cq = np.arange(GRID_W)[:, None]
    ck = np.arange(GRID_W)[None, :]
    c0 = np.clip(cq - 8, 0, GRID_W - 16)
    col_ok = (ck >= c0) & (ck < c0 + 16)
    col_off = np.clip(ck - cq + 15, 0, 30)
    t = jnp.take(rpb, jnp.asarray(col_off.reshape(-1), jnp.int32), axis=2).reshape(nh, 15, GRID_W, GRID_W)
    t = jnp.where(col_ok[None, None], t, NEG)
    tabs = []
    for case in range(3):
        per_row = []
        for i in range(4):
            lo, ro = ((0, 7 - i), (i, 3), (4, 3 - i))[case]
            blk = jnp.pad(t[:, ro:ro + 8], ((0, 0), (lo, NA_WIN_ROWS - 8 - lo), (0, 0), (0, 0)),
                          constant_values=NEG)
            per_row.append(blk.transpose(0, 2, 1, 3))
        tabs.append(jnp.stack(per_row, axis=1).reshape(nh, TM, NA_WIN_ROWS * GRID_W))
    return jnp.stack(tabs, axis=1)


def _na_kernel(q_ref, k_ref, v_ref, bias_ref, after_ref, o_ref, *, n_ctx, rows):
    del after_ref
    j = pl.program_id(1)
    n_ctx_tiles = n_ctx // TM
    scale = HEAD_DIM ** -0.5
    nwin = NA_WIN_ROWS * GRID_W

    def head(h):
        return slice(h * HEAD_DIM, (h + 1) * HEAD_DIM)

    @pl.when(j < n_ctx_tiles)
    def _ctx():
        for h in range(H_NA):
            q = q_ref[0, :, head(h)]
            s = _dot_nt(q, k_ref[0, 0:n_ctx, head(h)]) * scale
            p = jnp.exp(s - jnp.max(s, axis=-1, keepdims=True))
            y = jnp.dot(p.astype(BF16), v_ref[0, 0:n_ctx, head(h)], preferred_element_type=F32)
            o_ref[0, :, head(h)] = (y / jnp.sum(p, axis=-1, keepdims=True)).astype(BF16)

    @pl.when(j >= n_ctx_tiles)
    def _lat():
        m = j - n_ctx_tiles
        w0 = jnp.clip(4 * m - 4, 0, rows - NA_WIN_ROWS)
        start = pl.multiple_of(n_ctx + w0 * GRID_W, GRID_W)
        for h in range(H_NA):
            q = q_ref[0, :, head(h)]
            s_loc = _dot_nt(q, k_ref[0, pl.ds(start, nwin), head(h)]) * scale + bias_ref[h]
            s_ctx = _dot_nt(q, k_ref[0, 0:n_ctx, head(h)]) * scale
            mx = jnp.maximum(jnp.max(s_loc, axis=-1, keepdims=True), jnp.max(s_ctx, axis=-1, keepdims=True))
            p_loc = jnp.exp(s_loc - mx)
            p_ctx = jnp.exp(s_ctx - mx)
            den = jnp.sum(p_loc, axis=-1, keepdims=True) + jnp.sum(p_ctx, axis=-1, keepdims=True)
            y = (jnp.dot(p_loc.astype(BF16), v_ref[0, pl.ds(start, nwin), head(h)], preferred_element_type=F32)
                 + jnp.dot(p_ctx.astype(BF16), v_ref[0, 0:n_ctx, head(h)], preferred_element_type=F32))
            o_ref[0, :, head(h)] = (y / den).astype(BF16)


def _na(qkv, bias_tab, n_ctx, after):
    b, s, _ = qkv.shape
    nj = s // TM
    n_ctx_tiles = n_ctx // TM
    rows = (s - n_ctx) // GRID_W
    n_lat_tiles = nj - n_ctx_tiles
    assert rows >= NA_WIN_ROWS and rows % 4 == 0 and n_lat_tiles >= 3

    def case_map(b, j):
        m = j - n_ctx_tiles
        return (0, jnp.where(m <= 0, 0, jnp.where(m >= n_lat_tiles - 1, 2, 1)), 0, 0)

    return pl.pallas_call(
        functools.partial(_na_kernel, n_ctx=n_ctx, rows=rows),
        grid=(b, nj),
        in_specs=[pl.BlockSpec((1, TM, D_NA), lambda b, j: (b, j, 0)),
                  pl.BlockSpec((1, s, D_NA), lambda b, j: (b, 0, 1)),
                  pl.BlockSpec((1, s, D_NA), lambda b, j: (b, 0, 2)),
                  pl.BlockSpec((H_NA, None, TM, NA_WIN_ROWS * GRID_W), case_map),
                  pl.BlockSpec((1, 8, LANES), lambda b, j: (0, 0, 0))],
        out_specs=pl.BlockSpec((1, TM, D_NA), lambda b, j: (b, j, 0)),
        out_shape=jax.ShapeDtypeStruct((b, s, D_NA), BF16),
        compiler_params=_cparams(("arbitrary", "arbitrary")),
        name="na_attention",
    )(qkv, qkv, qkv, bias_tab, after)


def _feat_kernel(u_ref, up_ref, un_ref, mu_ref, w0_ref, w2_ref, a0_ref, a2_ref, g2_ref, kk_ref, ka_ref, rk_ref,
                 bd_ref, feat_ref, rd_ref, *, n_ctx_tiles, nj):
    j = pl.program_id(1)
    seg_start = (j == 0) | (j == n_ctx_tiles)
    seg_end = (j == n_ctx_tiles - 1) | (j == nj - 1)
    rid = lax.broadcasted_iota(jnp.int32, (TM, 1), 0)

    def shifted(lo, hi):
        p = u_ref[0, :, lo:hi]
        prev_row = jnp.where(seg_start, 0.0, up_ref[0, 7:8, lo:hi])
        next_row = jnp.where(seg_end, 0.0, un_ref[0, 0:1, lo:hi])
        prev = jnp.where(rid == 0, prev_row, pltpu.roll(p, 1, axis=0))
        nxt = jnp.where(rid == TM - 1, next_row, pltpu.roll(p, TM - 1, axis=0))
        return p + mu_ref[:, lo:hi] * (0.5 * (prev + nxt) - p)

    c = D_RWKV
    r = shifted(0, c)
    k = shifted(c, 2 * c)
    v = shifted(2 * c, 3 * c)
    o = 3 * c
    wl = shifted(o, o + 2 * R_DECAY)
    al = shifted(o + 2 * R_DECAY, o + 2 * R_DECAY + 2 * R_AAA)
    gl = shifted(o + 2 * R_DECAY + 2 * R_AAA, D_SHIFT)
    bd = bd_ref[...]

    lw = jnp.dot(jnp.tanh(wl), w2_ref[...], precision=HI, preferred_element_type=F32)
    la = jnp.dot(al, a2_ref[...], precision=HI, preferred_element_type=F32)
    g = jnp.dot(_sigmoid(gl), g2_ref[...], precision=HI, preferred_element_type=F32)

    kks = k * kk_ref[...]
    kk = kks / jnp.maximum(jnp.sqrt(_head_sum(kks * kks, bd)), L2_EPS)
    feat_ref[0, :, 0:c] = r
    feat_ref[0, :, c:2 * c] = v
    feat_ref[0, :, 2 * c:3 * c] = kk
    ksum = jnp.zeros_like(k)
    for z in range(2):
        w_log = -_softplus(-(w0_ref[z:z + 1, :] + lw[:, z * c:(z + 1) * c])) - 0.5
        decay = jnp.exp(-jnp.exp(w_log))
        a = _sigmoid(a0_ref[z:z + 1, :] + la[:, z * c:(z + 1) * c])
        k_dir = k * (1.0 + (a - 1.0) * ka_ref[...])
        ksum = ksum + k_dir
        base = (3 + 3 * z) * c
        feat_ref[0, :, base:base + c] = decay
        feat_ref[0, :, base + c:base + 2 * c] = k_dir
        feat_ref[0, :, base + 2 * c:base + 3 * c] = kk * a
    bonus = _head_sum(r * (0.5 * ksum) * rk_ref[...], bd) * v
    rd_ref[0, :, 0:c] = bonus
    rd_ref[0, :, c:2 * c] = g


def _features(u_raw, n_ctx, shift_mu, w0, w2blk, a0, a2blk, g2, k_k, k_a, r_k, bd):
    b, s, _ = u_raw.shape
    nj = s // TM
    n_ctx_tiles = n_ctx // TM
    c = D_RWKV
    t8 = TM // 8

    def full(shape):
        return pl.BlockSpec(shape, lambda b, j: (0,) * len(shape))

    return pl.pallas_call(
        functools.partial(_feat_kernel, n_ctx_tiles=n_ctx_tiles, nj=nj),
        grid=(b, nj),
        in_specs=[pl.BlockSpec((1, TM, D_SHIFT), lambda b, j: (b, j, 0)),
                  pl.BlockSpec((1, 8, D_SHIFT), lambda b, j: (b, jnp.maximum(j * t8 - 1, 0), 0)),
                  pl.BlockSpec((1, 8, D_SHIFT), lambda b, j: (b, jnp.minimum((j + 1) * t8, s // 8 - 1), 0)),
                  full((1, D_SHIFT)), full((2, c)), full((2 * R_DECAY, 2 * c)), full((2, c)),
                  full((2 * R_AAA, 2 * c)), full((R_GATE, c)), full((1, c)), full((1, c)), full((1, c)),
                  full((c, c))],
        out_specs=[pl.BlockSpec((1, TM, N_FEAT * c), lambda b, j: (b, j, 0)),
                   pl.BlockSpec((1, TM, 2 * c), lambda b, j: (b, j, 0))],
        out_shape=[jax.ShapeDtypeStruct((b, s, N_FEAT * c), F32),
                   jax.ShapeDtypeStruct((b, s, 2 * c), F32)],
        compiler_params=_cparams(("arbitrary", "arbitrary")),
        name="rwkv_features",
    )(u_raw, u_raw, u_raw, shift_mu.reshape(1, D_SHIFT), w0, w2blk, a0, a2blk, g2, k_k.reshape(1, c),
      k_a.reshape(1, c), r_k.reshape(1, c), bd)


def _scan_kernel(fs_ref, fd_ref, bs_ref, bd_ref, after_ref, yf_ref, yb_ref, st_ref, p_ref, op_ref):
    del after_ref

    @pl.when(pl.program_id(0) == 0)
    def _init():
        st_ref[...] = jnp.zeros_like(st_ref)

    kg = 8
    n_groups = HEAD_DIM // kg
    p_ref[...] = jnp.ones_like(p_ref)

    def tile(ref, row, a):
        return ref[row, a].reshape(HEAD_DIM, -1)

    def one_step(d, s_ref, d_ref, y_ref, row):
        v_t = tile(s_ref, row, 1)
        p_prev = p_ref[d]
        p_new = p_prev * tile(d_ref, row, 0)
        inv = 1.0 / p_new
        p_ref[d] = p_new
        op_ref[d, 0] = (tile(s_ref, row, 2) * p_prev).reshape(n_groups, kg, -1)
        op_ref[d, 1] = (tile(d_ref, row, 2) * inv).reshape(n_groups, kg, -1)
        op_ref[d, 2] = (tile(d_ref, row, 1) * inv).reshape(n_groups, kg, -1)
        op_ref[d, 3] = (tile(s_ref, row, 0) * p_new).reshape(n_groups, kg, -1)

        sa = jnp.zeros_like(v_t)
        for k in range(HEAD_DIM):
            sa = sa + st_ref[d, k] * op_ref[d, 0, k // kg, pl.ds(k % kg, 1), :]

        def pass_update(g, y):
            for kk in range(kg):
                k = g * kg + kk
                z_new = st_ref[d, k] - sa * op_ref[d, 1, g, pl.ds(kk, 1), :] + v_t * op_ref[d, 2, g, pl.ds(kk, 1), :]
                st_ref[d, k] = z_new
                y = y + z_new * op_ref[d, 3, g, pl.ds(kk, 1), :]
            return y

        y_ref[row] = lax.fori_loop(0, n_groups, pass_update, jnp.zeros_like(v_t))

    def step(jj, carry):
        one_step(0, fs_ref, fd_ref, yf_ref, jj)
        one_step(1, bs_ref, bd_ref, yb_ref, TC_SCAN - 1 - jj)
        return carry

    lax.fori_loop(0, TC_SCAN, step, 0)

    for d in range(2):
        for k in range(HEAD_DIM):
            st_ref[d, k] = st_ref[d, k] * p_ref[d, pl.ds(k, 1), :]


def _scan(feat_t, n_ctx, after):
    s, _, hd, lanes = feat_t.shape
    feat_t = feat_t.reshape(s, N_FEAT, hd // 8, 8, lanes)
    nblk = s // TC_SCAN
    nc = n_ctx // TC_SCAN
    assert n_ctx % TC_SCAN == 0 and s % TC_SCAN == 0

    def bwd(i):
        return jnp.where(i < nc, nc - 1 - i, nblk - 1 - i + nc)

    blk = (TC_SCAN, 3, hd // 8, 8, lanes)
    yblk = (TC_SCAN, hd, lanes)
    return pl.pallas_call(
        _scan_kernel,
        grid=(nblk,),
        in_specs=[pl.BlockSpec(blk, lambda i: (i, 0, 0, 0, 0)),
                  pl.BlockSpec(blk, lambda i: (i, 1, 0, 0, 0)),
                  pl.BlockSpec(blk, lambda i: (bwd(i), 0, 0, 0, 0)),
                  pl.BlockSpec(blk, lambda i: (bwd(i), 2, 0, 0, 0)),
                  pl.BlockSpec((1, 16, after.shape[2]), lambda i: (0, 0, 0))],
        out_specs=[pl.BlockSpec(yblk, lambda i: (i, 0, 0)),
                   pl.BlockSpec(yblk, lambda i: (bwd(i), 0, 0))],
        out_shape=[jax.ShapeDtypeStruct((s, hd, lanes), F32)] * 2,
        scratch_shapes=[pltpu.VMEM((2, HEAD_DIM, hd, lanes), F32),
                        pltpu.VMEM((2, hd, lanes), F32),
                        pltpu.VMEM((2, 4, hd // 8, 8, lanes), F32)],
        compiler_params=_cparams(("arbitrary",)),
        name="rwkv_scan",
    )(feat_t, feat_t, feat_t, feat_t, after)


def _outproj_kernel(yf_ref, yb_ref, rd_ref, yna_ref, x_ref, mod_ref, lnw_ref, lnb_ref, bd_ref, w_ref, o_ref):
    c = D_RWKV
    bd = bd_ref[...]
    y = yf_ref[0] + yb_ref[0]
    mu = _head_sum(y, bd) * (1.0 / HEAD_DIM)
    dlt = y - mu
    var = _head_sum(dlt * dlt, bd) * (1.0 / HEAD_DIM)
    yn = dlt * lax.rsqrt(var + GN_EPS) * lnw_ref[...] + lnb_ref[...]
    y_rw = ((yn + rd_ref[0, :, 0:c]) * rd_ref[0, :, c:2 * c]).astype(BF16)
    o = (jnp.dot(y_rw, w_ref[0:c, :], preferred_element_type=F32)
         + jnp.dot(yna_ref[0], w_ref[c:, :], preferred_element_type=F32))
    o_ref[0] = x_ref[0] + mod_ref[2:3, :] * o


def _outproj(yf, yb, rd, yna, xs, modtab, ln_w, ln_b, bd, w_out_bf, n_ctx_tiles):
    b, s, d = xs.shape
    nj = s // TM
    c = D_RWKV

    def tile(w):
        return pl.BlockSpec((1, TM, w), lambda b, j: (b, j, 0))

    def full(shape):
        return pl.BlockSpec(shape, lambda b, j: (0,) * len(shape))

    return pl.pallas_call(
        _outproj_kernel,
        grid=(b, nj),
        in_specs=[tile(c), tile(c), tile(2 * c), tile(D_NA), tile(d), _mod_spec(n_ctx_tiles),
                  full((1, c)), full((1, c)), full((c, c)), full((d, d))],
        out_specs=tile(d),
        out_shape=jax.ShapeDtypeStruct((b, s, d), F32),
        compiler_params=_cparams(("arbitrary", "arbitrary")),
        name="out_proj",
    )(yf, yb, rd, yna, xs, modtab, ln_w.reshape(1, c), ln_b.reshape(1, c), bd, w_out_bf)


def _ffn_kernel(x_ref, mod_ref, g_ref, w1_ref, w3_ref, w2_ref, o_ref, *, ffc):
    x = x_ref[0]
    h = _norm_mod(x, g_ref[...], mod_ref[4:5, :], mod_ref[3:4, :]).astype(BF16)
    d_ff = w1_ref.shape[1]
    acc = jnp.zeros(x.shape, F32)
    for c0 in range(0, d_ff, ffc):
        a = jnp.dot(h, w1_ref[:, c0:c0 + ffc], preferred_element_type=F32)
        g = jnp.dot(h, w3_ref[:, c0:c0 + ffc], preferred_element_type=F32)
        acc = acc + jnp.dot((_silu(a) * g).astype(BF16), w2_ref[c0:c0 + ffc, :], preferred_element_type=F32)
    o_ref[0] = x + mod_ref[5:6, :] * acc


def _ffn(xs, modtab, g, w1, w3, w2, n_ctx_tiles):
    b, s, d = xs.shape
    nj = s // TM
    d_ff = w1.shape[1]
    const = lambda b, j: (0, 0)
    return pl.pallas_call(
        functools.partial(_ffn_kernel, ffc=d_ff // 2),
        grid=(b, nj),
        in_specs=[pl.BlockSpec((1, TM, d), lambda b, j: (b, j, 0)), _mod_spec(n_ctx_tiles),
                  pl.BlockSpec((1, d), const),
                  pl.BlockSpec((d, d_ff), const), pl.BlockSpec((d, d_ff), const), pl.BlockSpec((d_ff, d), const)],
        out_specs=pl.BlockSpec((1, TM, d), lambda b, j: (b, j, 0)),
        out_shape=jax.ShapeDtypeStruct((b, s, d), F32),
        compiler_params=_cparams(("arbitrary", "arbitrary"), 56),
        name="ffn_swiglu",
    )(xs, modtab, g.reshape(1, d), w1, w3, w2)


def _router_kernel(x_ref, mod_ref, g_ref, r_ref, h_ref, slab_ref):
    h = _norm_mod(x_ref[0], g_ref[...], mod_ref[4:5, :], mod_ref[3:4, :])
    h_ref[0] = h
    logits = jnp.dot(h, r_ref[...], precision=HI, preferred_element_type=F32)
    lane = lax.broadcasted_iota(jnp.int32, logits.shape, 1)
    lg = jnp.where(lane < N_EXPERTS, logits, -jnp.inf)
    m1 = jnp.max(lg, axis=-1, keepdims=True)
    i1 = jnp.min(jnp.where(lg == m1, lane, LANES), axis=-1, keepdims=True)
    lg2 = jnp.where(lane == i1, -jnp.inf, lg)
    m2 = jnp.max(lg2, axis=-1, keepdims=True)
    i2 = jnp.min(jnp.where(lg2 == m2, lane, LANES), axis=-1, keepdims=True)
    e = jnp.exp(m2 - m1)
    g1 = 1.0 / (1.0 + e)
    g2 = e / (1.0 + e)
    slab_ref[0] = jnp.where(lane == 0, i1.astype(F32),
                            jnp.where(lane == 1, i2.astype(F32),
                                      jnp.where(lane == 2, g1, jnp.where(lane == 3, g2, 0.0))))


def _router(xs, modtab, g, router_pad, n_ctx_tiles):
    b, s, d = xs.shape
    nj = s // TM
    const = lambda b, j: (0, 0)
    return pl.pallas_call(
        _router_kernel,
        grid=(b, nj),
        in_specs=[pl.BlockSpec((1, TM, d), lambda b, j: (b, j, 0)), _mod_spec(n_ctx_tiles),
                  pl.BlockSpec((1, d), const), pl.BlockSpec((d, LANES), const)],
        out_specs=[pl.BlockSpec((1, TM, d), lambda b, j: (b, j, 0)),
                   pl.BlockSpec((1, TM, LANES), lambda b, j: (b, j, 0))],
        out_shape=[jax.ShapeDtypeStruct((b, s, d), F32), jax.ShapeDtypeStruct((b, s, LANES), F32)],
        compiler_params=_cparams(("arbitrary", "arbitrary")),
        name="moe_router",
    )(xs, modtab, g.reshape(1, d), router_pad)


def _dispatch(idx):
    n = idx.shape[0]
    a = n * 2
    flat_e = idx.reshape(a)
    onehot = (flat_e[None, :] == jnp.arange(N_EXPERTS, dtype=jnp.int32)[:, None]).astype(jnp.int32)
    csum = jnp.cumsum(onehot, axis=1)
    counts = csum[:, -1]
    rank = jnp.sum(csum * onehot, axis=0) - 1
    padded = (counts + MOE_BLOCK - 1) // MOE_BLOCK * MOE_BLOCK
    pad_ends = jnp.cumsum(padded)
    pad_starts = pad_ends - padded
    dest = (pad_starts[flat_e] + rank).astype(jnp.int32)
    n_blocks = -(-a // MOE_BLOCK) + N_EXPERTS
    slot_tok = jnp.zeros((n_blocks * MOE_BLOCK,), jnp.int32).at[dest].set(jnp.arange(a, dtype=jnp.int32) // 2)
    block_start = jnp.arange(n_blocks, dtype=jnp.int32) * MOE_BLOCK
    block_e = jnp.minimum(jnp.sum((pad_ends[None, :] <= block_start[:, None]).astype(jnp.int32), axis=1),
                          N_EXPERTS - 1)
    n_used = (pad_ends[-1] // MOE_BLOCK).astype(jnp.int32).reshape(1)
    return slot_tok, block_e, n_used, dest


def _moe_mm_kernel(be_ref, st_ref, nu_ref, h_hbm, w1_hbm, w3_hbm, w2_hbm, o_ref, xbuf, w1s, w3s, w2s, stage_cols,
                   stage_rows, gsem, wsem,
                   *, ffc, layer):
    i = pl.program_id(0)
    n_used = nu_ref[0]

    def row_copy(tok, slot, r):
        return pltpu.make_async_copy(h_hbm.at[pl.ds(tok, 1)], xbuf.at[slot, pl.ds(r, 1)], gsem.at[slot])

    def issue(blk, slot):
        def body(r, carry):
            row_copy(st_ref[blk * MOE_BLOCK + r], slot, r).start()
            return carry
        lax.fori_loop(0, MOE_BLOCK, body, 0, unroll=8)

    def wait_rows(slot):
        for r in range(MOE_BLOCK):
            row_copy(0, slot, r).wait()

    n_blocks = pl.num_programs(0)
    slot = lax.rem(i, 2)
    nslot = 1 - slot
    nxt = jnp.minimum(i + 1, n_blocks - 1)

    @pl.when(i == 0)
    def _first():
        issue(0, 0)

    @pl.when(i < n_used)
    def _compute():
        e = be_ref[i]
        changed = (i == 0) | (e != be_ref[jnp.maximum(i - 1, 0)])

        @pl.when(changed)
        def _load_weights():
            nch = w1s.shape[1] // ffc
            chunks = [(m, c) for m in range(3) for c in range(nch)]

            def chunk_copy(n):
                m, c = chunks[n]
                if m < 2:
                    src = (w1_hbm, w3_hbm)[m].at[layer, e, :, pl.ds(c * ffc, ffc)]
                    return pltpu.make_async_copy(src, stage_cols.at[n % 2], wsem.at[n % 2])
                return pltpu.make_async_copy(w2_hbm.at[layer, e, pl.ds(c * ffc, ffc), :], stage_rows.at[n % 2],
                                             wsem.at[n % 2])

            chunk_copy(0).start()
            for n, (m, c) in enumerate(chunks):
                if n + 1 < len(chunks):
                    chunk_copy(n + 1).start()
                chunk_copy(n).wait()
                if m < 2:
                    (w1s, w3s)[m][:, c * ffc:(c + 1) * ffc] = stage_cols[n % 2].astype(BF16)
                else:
                    w2s[c * ffc:(c + 1) * ffc, :] = stage_rows[n % 2].astype(BF16)

        wait_rows(slot)
        xb = xbuf[slot].astype(BF16)
        acc = jnp.zeros(o_ref.shape, F32)
        d_ff = w1s.shape[1]
        n_chunks = d_ff // ffc
        rows_per_chunk = -(-MOE_BLOCK // n_chunks)
        for ci in range(n_chunks):
            c0 = ci * ffc
            a = jnp.dot(xb, w1s[:, c0:c0 + ffc], preferred_element_type=F32)
            g = jnp.dot(xb, w3s[:, c0:c0 + ffc], preferred_element_type=F32)
            acc = acc + jnp.dot((_silu(a) * g).astype(BF16), w2s[c0:c0 + ffc, :], preferred_element_type=F32)
            for r in range(ci * rows_per_chunk, min((ci + 1) * rows_per_chunk, MOE_BLOCK)):
                row_copy(st_ref[nxt * MOE_BLOCK + r], nslot, r).start()
        o_ref[...] = acc

    @pl.when(i >= n_used)
    def _unused():
        o_ref[...] = jnp.zeros_like(o_ref)
        wait_rows(slot)
        issue(nxt, nslot)

    @pl.when(i == n_blocks - 1)
    def _drain():
        wait_rows(nslot)


def _moe_mm(h_flat, slot_tok, block_e, n_used, w1, w3, w2, layer):
    n, d = h_flat.shape
    n_blocks = block_e.shape[0]
    d_ff = w1.shape[3]
    any_spec = pl.BlockSpec(memory_space=pl.ANY)
    return pl.pallas_call(
        functools.partial(_moe_mm_kernel, ffc=MOE_FF_CHUNK, layer=layer),
        grid_spec=pltpu.PrefetchScalarGridSpec(
            num_scalar_prefetch=3,
            grid=(n_blocks,),
            in_specs=[any_spec, any_spec, any_spec, any_spec],
            out_specs=pl.BlockSpec((MOE_BLOCK, d), lambda i, be, st, nu: (i, 0)),
            scratch_shapes=[pltpu.VMEM((2, MOE_BLOCK, d), F32),
                            pltpu.VMEM((d, d_ff), BF16), pltpu.VMEM((d, d_ff), BF16), pltpu.VMEM((d_ff, d), BF16),
                            pltpu.VMEM((2, d, MOE_FF_CHUNK), F32), pltpu.VMEM((2, MOE_FF_CHUNK, d), F32),
                            pltpu.SemaphoreType.DMA((2,)), pltpu.SemaphoreType.DMA((2,))]),
        out_shape=jax.ShapeDtypeStruct((n_blocks * MOE_BLOCK, d), F32),
        compiler_params=_cparams(("arbitrary",), 56),
        name="moe_grouped_swiglu",
    )(block_e, slot_tok, n_used, h_flat, w1, w3, w2)


def _moe_comb_kernel(inv_ref, x_ref, mod_ref, slab_ref, fg_ref, yb_hbm, o_ref, gbuf, gsem, *, nj, final):
    b = pl.program_id(0)
    j = pl.program_id(1)
    t = b * nj + j
    nt = pl.num_programs(0) * nj

    def row_copy(src, slot, which, r):
        return pltpu.make_async_copy(yb_hbm.at[pl.ds(src, 1)], gbuf.at[slot, which, pl.ds(r, 1)], gsem.at[slot])

    def issue(tile, slot):
        def body(r, carry):
            base = (tile * TM + r) * 2
            row_copy(inv_ref[base], slot, 0, r).start()
            row_copy(inv_ref[base + 1], slot, 1, r).start()
            return carry
        lax.fori_loop(0, TM, body, 0, unroll=8)

    def wait_rows(slot):
        for r in range(TM):
            row_copy(0, slot, 0, r).wait()
            row_copy(0, slot, 1, r).wait()

    @pl.when(t == 0)
    def _first():
        issue(0, 0)

    @pl.when(t + 1 < nt)
    def _prefetch():
        issue(t + 1, lax.rem(t + 1, 2))

    slot = lax.rem(t, 2)
    wait_rows(slot)
    slab = slab_ref[0]
    y = slab[:, 2:3] * gbuf[slot, 0] + slab[:, 3:4] * gbuf[slot, 1]
    xn = x_ref[0] + mod_ref[5:6, :] * y
    if final:
        xn = _rms(xn) * fg_ref[...]
    o_ref[0] = xn


def _moe_combine(inv, xs, modtab, slab, final_g, yb, n_ctx_tiles, final):
    b, s, d = xs.shape
    nj = s // TM
    return pl.pallas_call(
        functools.partial(_moe_comb_kernel, nj=nj, final=final),
        grid_spec=pltpu.PrefetchScalarGridSpec(
            num_scalar_prefetch=1,
            grid=(b, nj),
            in_specs=[pl.BlockSpec((1, TM, d), lambda b, j, inv: (b, j, 0)),
                      pl.BlockSpec((None, None, 6, D_MODEL),
                                   lambda b, j, inv: (b, jnp.where(j >= n_ctx_tiles, 1, 0), 0, 0)),
                      pl.BlockSpec((1, TM, LANES), lambda b, j, inv: (b, j, 0)),
                      pl.BlockSpec((1, d), lambda b, j, inv: (0, 0)),
                      pl.BlockSpec(memory_space=pl.ANY)],
            out_specs=pl.BlockSpec((1, TM, d), lambda b, j, inv: (b, j, 0)),
            scratch_shapes=[pltpu.VMEM((2, 2, TM, d), F32), pltpu.SemaphoreType.DMA((2,))]),
        out_shape=jax.ShapeDtypeStruct((b, s, d), F32),
        compiler_params=_cparams(("arbitrary", "arbitrary")),
        name="moe_combine",
    )(inv, xs, modtab, slab, final_g.reshape(1, d), yb)


def _blockdiag(w):
    z = jnp.zeros_like(w[0])
    return jnp.concatenate([jnp.concatenate([w[0], z], axis=1), jnp.concatenate([z, w[1]], axis=1)], axis=0)


def kernel(x, c, ctx, c_ctx, ada_w, ada_b, norm_mix_g, norm_ffn_g, w_in, shift_mu, w0, w2, a0, a2, g2, k_k, k_a, r_k,
           ln_x_w, ln_x_b, na_rpb, w_out, ffn_w1, ffn_w3, ffn_w2, router, moe_w1, moe_w3, moe_w2, final_g):
    b, t, d = x.shape
    n_ctx = ctx.shape[1]
    s = n_ctx + t
    depth = ada_w.shape[0]
    assert d == D_MODEL and n_ctx % TM == 0 and t % TM == 0
    n_ctx_tiles = n_ctx // TM
    lanes = b * H_RWKV

    nb = -(-(b + 1) // 8) * 8
    c_all = jnp.concatenate([c, c_ctx[None, :], jnp.zeros((nb - b - 1, d), F32)], axis=0)
    mods = _mods(c_all, ada_w, ada_b).reshape(depth, nb, 6, d)
    modtab = jnp.stack([jnp.broadcast_to(mods[:, b:b + 1], (depth, b, 6, d)), mods[:, :b]], axis=2)

    head_id = np.arange(D_RWKV) // HEAD_DIM
    bd = jnp.asarray(head_id[:, None] == head_id[None, :], BF16)

    xs = jnp.concatenate([ctx, x], axis=1)
    for l in range(depth):
        last = l == depth - 1
        qkv, u_raw = _inproj(xs, modtab[l], norm_mix_g[l], w_in[l].astype(BF16), n_ctx_tiles)
        feat, rd = _features(u_raw, n_ctx, shift_mu[l], w0[l], _blockdiag(w2[l]), a0[l], _blockdiag(a2[l]), g2[l],
                             k_k[l], k_a[l], r_k[l], bd)
        y_na = _na(qkv, _na_bias_table(na_rpb[l]), n_ctx, rd)
        feat_t = feat.reshape(b, s, N_FEAT, H_RWKV, HEAD_DIM).transpose(1, 2, 4, 0, 3).reshape(
            s, N_FEAT, HEAD_DIM, lanes)
        yf_t, yb_t = _scan(feat_t, n_ctx, y_na)

        def untranspose(y):
            return y.reshape(s, HEAD_DIM, b, H_RWKV).transpose(2, 0, 3, 1).reshape(b, s, D_RWKV)

        xs = _outproj(untranspose(yf_t), untranspose(yb_t), rd, y_na, xs, modtab[l], ln_x_w[l], ln_x_b[l], bd,
                      w_out[l].astype(BF16), n_ctx_tiles)
        i = l // 2
        if l % 2 == 0:
            xs = _ffn(xs, modtab[l], norm_ffn_g[l], ffn_w1[i].astype(BF16), ffn_w3[i].astype(BF16),
                      ffn_w2[i].astype(BF16), n_ctx_tiles)
        else:
            xm, nct = (xs[:, n_ctx:], 0) if last else (xs, n_ctx_tiles)
            n_tok = xm.shape[0] * xm.shape[1]
            router_pad = jnp.pad(router[i], ((0, 0), (0, LANES - N_EXPERTS)))
            h, slab = _router(xm, modtab[l], norm_ffn_g[l], router_pad, nct)
            idx = slab[..., 0:2].astype(jnp.int32).reshape(n_tok, 2)
            slot_tok, block_e, n_used, dest = _dispatch(idx)
            yb = _moe_mm(h.reshape(n_tok, d), slot_tok, block_e, n_used, moe_w1, moe_w3, moe_w2, i)
            xs = _moe_combine(dest, xm, modtab[l], slab, final_g, yb, nct, final=last)
    if depth % 2 == 1:
        raise NotImplementedError("final norm is fused into the MoE combine of the last (odd) layer")
    return xs
```

```python
import functools

import numpy as np
import jax
import jax.numpy as jnp
from jax import lax
from jax.experimental import pallas as pl
from jax.experimental.pallas import tpu as pltpu

F32 = jnp.float32
BF16 = jnp.bfloat16
HI = lax.Precision.HIGHEST

D_MODEL = 1024
HEAD_DIM = 64
GRID_W = 64
D_RWKV = 512
D_NA = 512
H_RWKV = D_RWKV // HEAD_DIM
H_NA = D_NA // HEAD_DIM
R_DECAY = 64
R_AAA = 64
R_GATE = 128
D_SHIFT = 3 * D_RWKV + 2 * R_DECAY + 2 * R_AAA + R_GATE
N_QKV = 3 * D_NA
N_EXPERTS = 8
MOE_BLOCK = 256
RMS_EPS = 1e-6
GN_EPS = 64e-5
L2_EPS = 1e-12
NEG = -1e30

TM = 256
NA_WIN_ROWS = 12
N_FEAT = 9
TC_SCAN = 32
MOE_FF_CHUNK = 512
LANES = 128


def _cparams(sem, vmem_mb=48):
    return pltpu.CompilerParams(dimension_semantics=sem, vmem_limit_bytes=vmem_mb * 1024 * 1024)


def _sigmoid(x):
    return 1.0 / (1.0 + jnp.exp(-x))


def _silu(x):
    return x * _sigmoid(x)


def _softplus(x):
    return jnp.maximum(x, 0.0) + jnp.log(1.0 + jnp.exp(-jnp.abs(x)))


def _rms(x):
    return x * lax.rsqrt(jnp.mean(x * x, axis=-1, keepdims=True) + RMS_EPS)


def _norm_mod(x, g, scale, shift):
    return _rms(x) * g * (1.0 + scale) + shift


def _head_sum(x, bd):
    hi = x.astype(BF16)
    lo = (x - hi.astype(F32)).astype(BF16)
    return jnp.dot(hi, bd, preferred_element_type=F32) + jnp.dot(lo, bd, preferred_element_type=F32)


def _dot_nt(a, b):
    return lax.dot_general(a, b, (((1,), (1,)), ((), ())), preferred_element_type=F32)


def _mods_kernel(c_ref, w_ref, b_ref, o_ref):
    o_ref[0] = jnp.dot(_silu(c_ref[...]), w_ref[0], precision=HI, preferred_element_type=F32) + b_ref[0]


def _mods(c_all, ada_w, ada_b):
    depth, d, n6 = ada_w.shape
    nb = c_all.shape[0]
    tn = 1536
    return pl.pallas_call(
        _mods_kernel,
        grid=(depth, n6 // tn),
        in_specs=[pl.BlockSpec((nb, d), lambda l, n: (0, 0)),
                  pl.BlockSpec((1, d, tn), lambda l, n: (l, 0, n)),
                  pl.BlockSpec((1, 1, tn), lambda l, n: (l, 0, n))],
        out_specs=pl.BlockSpec((1, nb, tn), lambda l, n: (l, 0, n)),
        out_shape=jax.ShapeDtypeStruct((depth, nb, n6), F32),
        compiler_params=_cparams(("arbitrary", "arbitrary")),
        name="adaln_mods",
    )(c_all, ada_w, ada_b.reshape(depth, 1, n6))


def _mod_spec(n_ctx_tiles):
    return pl.BlockSpec((None, None, 6, D_MODEL), lambda b, j: (b, jnp.where(j >= n_ctx_tiles, 1, 0), 0, 0))


def _inproj_kernel(x_ref, mod_ref, g_ref, w_ref, qkv_ref, u_ref):
    h = _norm_mod(x_ref[0], g_ref[...], mod_ref[1:2, :], mod_ref[0:1, :]).astype(BF16)
    qkv_ref[0] = jnp.dot(h, w_ref[:, :N_QKV], preferred_element_type=F32).astype(BF16)
    u_ref[0] = jnp.dot(h, w_ref[:, N_QKV:], preferred_element_type=F32)


def _inproj(xs, modtab, g, w_in_bf, n_ctx_tiles):
    b, s, d = xs.shape
    nj = s // TM
    return pl.pallas_call(
        _inproj_kernel,
        grid=(b, nj),
        in_specs=[pl.BlockSpec((1, TM, d), lambda b, j: (b, j, 0)),
                  _mod_spec(n_ctx_tiles),
                  pl.BlockSpec((1, d), lambda b, j: (0, 0)),
                  pl.BlockSpec((d, N_QKV + D_SHIFT), lambda b, j: (0, 0))],
        out_specs=[pl.BlockSpec((1, TM, N_QKV), lambda b, j: (b, j, 0)),
                   pl.BlockSpec((1, TM, D_SHIFT), lambda b, j: (b, j, 0))],
        out_shape=[jax.ShapeDtypeStruct((b, s, N_QKV), BF16),
                   jax.ShapeDtypeStruct((b, s, D_SHIFT), F32)],
        compiler_params=_cparams(("arbitrary", "arbitrary")),
        name="in_proj",
    )(xs, modtab, g.reshape(1, d), w_in_bf)


def _na_bias_table(rpb):
    nh = rpb.shape[0]
    cq = np.arange(GRID_W)[:, None]
    ck = np.arange(GRID_W)[None, :]
    c0 = np.clip(cq - 8, 0, GRID_W - 16)
    col_ok = (ck >= c0) & (ck < c0 + 16)
    col_off = np.clip(ck - cq + 15, 0, 30)
    t = jnp.take(rpb, jnp.asarray(col_off.reshape(-1), jnp.int32), axis=2).reshape(nh, 15, GRID_W, GRID_W)
    t = jnp.where(col_ok[None, None], t, NEG)
    tabs = []
    for case in range(3):
        per_row = []
        for i in range(4):
            lo, ro = ((0, 7 - i), (i, 3), (4, 3 - i))[case]
            blk = jnp.pad(t[:, ro:ro + 8], ((0, 0), (lo, NA_WIN_ROWS - 8 - lo), (0, 0), (0, 0)),
                          constant_values=NEG)
            per_row.append(blk.transpose(0, 2, 1, 3))
        tabs.append(jnp.stack(per_row, axis=1).reshape(nh, TM, NA_WIN_ROWS * GRID_W))
    return jnp.stack(tabs, axis=1)


def _na_kernel(q_ref, k_ref, v_ref, bias_ref, after_ref, o_ref, *, n_ctx, rows):
    del after_ref
    j = pl.program_id(1)
    n_ctx_tiles = n_ctx // TM
    scale = HEAD_DIM ** -0.5
    nwin = NA_WIN_ROWS * GRID_W

    def head(h):
        return slice(h * HEAD_DIM, (h + 1) * HEAD_DIM)

    @pl.when(j < n_ctx_tiles)
    def _ctx():
        for h in range(H_NA):
            q = q_ref[0, :, head(h)]
            s = _dot_nt(q, k_ref[0, 0:n_ctx, head(h)]) * scale
            p = jnp.exp(s - jnp.max(s, axis=-1, keepdims=True))
            y = jnp.dot(p.astype(BF16), v_ref[0, 0:n_ctx, head(h)], preferred_element_type=F32)
            o_ref[0, :, head(h)] = (y / jnp.sum(p, axis=-1, keepdims=True)).astype(BF16)

    @pl.when(j >= n_ctx_tiles)
    def _lat():
        m = j - n_ctx_tiles
        w0 = jnp.clip(4 * m - 4, 0, rows - NA_WIN_ROWS)
        start = pl.multiple_of(n_ctx + w0 * GRID_W, GRID_W)
        for h in range(H_NA):
            q = q_ref[0, :, head(h)]
            s_loc = _dot_nt(q, k_ref[0, pl.ds(start, nwin), head(h)]) * scale + bias_ref[h]
            s_ctx = _dot_nt(q, k_ref[0, 0:n_ctx, head(h)]) * scale
            mx = jnp.maximum(jnp.max(s_loc, axis=-1, keepdims=True), jnp.max(s_ctx, axis=-1, keepdims=True))
            p_loc = jnp.exp(s_loc - mx)
            p_ctx = jnp.exp(s_ctx - mx)
            den = jnp.sum(p_loc, axis=-1, keepdims=True) + jnp.sum(p_ctx, axis=-1, keepdims=True)
            y = (jnp.dot(p_loc.astype(BF16), v_ref[0, pl.ds(start, nwin), head(h)], preferred_element_type=F32)
                 + jnp.dot(p_ctx.astype(BF16), v_ref[0, 0:n_ctx, head(h)], preferred_element_type=F32))
            o_ref[0, :, head(h)] = (y / den).astype(BF16)


def _na(qkv, bias_tab, n_ctx, after):
    b, s, _ = qkv.shape
    nj = s // TM
    n_ctx_tiles = n_ctx // TM
    rows = (s - n_ctx) // GRID_W
    n_lat_tiles = nj - n_ctx_tiles
    assert rows >= NA_WIN_ROWS and rows % 4 == 0 and n_lat_tiles >= 3

    def case_map(b, j):
        m = j - n_ctx_tiles
        return (0, jnp.where(m <= 0, 0, jnp.where(m >= n_lat_tiles - 1, 2, 1)), 0, 0)

    return pl.pallas_call(
        functools.partial(_na_kernel, n_ctx=n_ctx, rows=rows),
        grid=(b, nj),
        in_specs=[pl.BlockSpec((1, TM, D_NA), lambda b, j: (b, j, 0)),
                  pl.BlockSpec((1, s, D_NA), lambda b, j: (b, 0, 1)),
                  pl.BlockSpec((1, s, D_NA), lambda b, j: (b, 0, 2)),
                  pl.BlockSpec((H_NA, None, TM, NA_WIN_ROWS * GRID_W), case_map),
                  pl.BlockSpec((1, 8, LANES), lambda b, j: (0, 0, 0))],
        out_specs=pl.BlockSpec((1, TM, D_NA), lambda b, j: (b, j, 0)),
        out_shape=jax.ShapeDtypeStruct((b, s, D_NA), BF16),
        compiler_params=_cparams(("arbitrary", "arbitrary")),
        name="na_attention",
    )(qkv, qkv, qkv, bias_tab, after)


def _feat_kernel(u_ref, up_ref, un_ref, mu_ref, w0_ref, w2_ref, a0_ref, a2_ref, g2_ref, kk_ref, ka_ref, rk_ref,
                 bd_ref, feat_ref, rd_ref, *, n_ctx_tiles, nj):
    j = pl.program_id(1)
    seg_start = (j == 0) | (j == n_ctx_tiles)
    seg_end = (j == n_ctx_tiles - 1) | (j == nj - 1)
    rid = lax.broadcasted_iota(jnp.int32, (TM, 1), 0)

    def shifted(lo, hi):
        p = u_ref[0, :, lo:hi]
        prev_row = jnp.where(seg_start, 0.0, up_ref[0, 7:8, lo:hi])
        next_row = jnp.where(seg_end, 0.0, un_ref[0, 0:1, lo:hi])
        prev = jnp.where(rid == 0, prev_row, pltpu.roll(p, 1, axis=0))
        nxt = jnp.where(rid == TM - 1, next_row, pltpu.roll(p, TM - 1, axis=0))
        return p + mu_ref[:, lo:hi] * (0.5 * (prev + nxt) - p)

    c = D_RWKV
    r = shifted(0, c)
    k = shifted(c, 2 * c)
    v = shifted(2 * c, 3 * c)
    o = 3 * c
    wl = shifted(o, o + 2 * R_DECAY)
    al = shifted(o + 2 * R_DECAY, o + 2 * R_DECAY + 2 * R_AAA)
    gl = shifted(o + 2 * R_DECAY + 2 * R_AAA, D_SHIFT)
    bd = bd_ref[...]

    lw = jnp.dot(jnp.tanh(wl), w2_ref[...], precision=HI, preferred_element_type=F32)
    la = jnp.dot(al, a2_ref[...], precision=HI, preferred_element_type=F32)
    g = jnp.dot(_sigmoid(gl), g2_ref[...], precision=HI, preferred_element_type=F32)

    kks = k * kk_ref[...]
    kk = kks / jnp.maximum(jnp.sqrt(_head_sum(kks * kks, bd)), L2_EPS)
    feat_ref[0, :, 0:c] = r
    feat_ref[0, :, c:2 * c] = v
    feat_ref[0, :, 2 * c:3 * c] = kk
    ksum = jnp.zeros_like(k)
    for z in range(2):
        w_log = -_softplus(-(w0_ref[z:z + 1, :] + lw[:, z * c:(z + 1) * c])) - 0.5
        decay = jnp.exp(-jnp.exp(w_log))
        a = _sigmoid(a0_ref[z:z + 1, :] + la[:, z * c:(z + 1) * c])
        k_dir = k * (1.0 + (a - 1.0) * ka_ref[...])
        ksum = ksum + k_dir
        base = (3 + 3 * z) * c
        feat_ref[0, :, base:base + c] = decay
        feat_ref[0, :, base + c:base + 2 * c] = k_dir
        feat_ref[0, :, base + 2 * c:base + 3 * c] = kk * a
    bonus = _head_sum(r * (0.5 * ksum) * rk_ref[...], bd) * v
    rd_ref[0, :, 0:c] = bonus
    rd_ref[0, :, c:2 * c] = g


def _features(u_raw, n_ctx, shift_mu, w0, w2blk, a0, a2blk, g2, k_k, k_a, r_k, bd):
    b, s, _ = u_raw.shape
    nj = s // TM
    n_ctx_tiles = n_ctx // TM
    c = D_RWKV
    t8 = TM // 8

    def full(shape):
        return pl.BlockSpec(shape, lambda b, j: (0,) * len(shape))

    return pl.pallas_call(
        functools.partial(_feat_kernel, n_ctx_tiles=n_ctx_tiles, nj=nj),
        grid=(b, nj),
        in_specs=[pl.BlockSpec((1, TM, D_SHIFT), lambda b, j: (b, j, 0)),
                  pl.BlockSpec((1, 8, D_SHIFT), lambda b, j: (b, jnp.maximum(j * t8 - 1, 0), 0)),
                  pl.BlockSpec((1, 8, D_SHIFT), lambda b, j: (b, jnp.minimum((j + 1) * t8, s // 8 - 1), 0)),
                  full((1, D_SHIFT)), full((2, c)), full((2 * R_DECAY, 2 * c)), full((2, c)),
                  full((2 * R_AAA, 2 * c)), full((R_GATE, c)), full((1, c)), full((1, c)), full((1, c)),
                  full((c, c))],
        out_specs=[pl.BlockSpec((1, TM, N_FEAT * c), lambda b, j: (b, j, 0)),
                   pl.BlockSpec((1, TM, 2 * c), lambda b, j: (b, j, 0))],
        out_shape=[jax.ShapeDtypeStruct((b, s, N_FEAT * c), F32),
                   jax.ShapeDtypeStruct((b, s, 2 * c), F32)],
        compiler_params=_cparams(("arbitrary", "arbitrary")),
        name="rwkv_features",
    )(u_raw, u_raw, u_raw, shift_mu.reshape(1, D_SHIFT), w0, w2blk, a0, a2blk, g2, k_k.reshape(1, c),
      k_a.reshape(1, c), r_k.reshape(1, c), bd)


def _scan_kernel(fs_ref, fd_ref, bs_ref, bd_ref, after_ref, yf_ref, yb_ref, st_ref, p_ref, op_ref):
    del after_ref

    @pl.when(pl.program_id(0) == 0)
    def _init():
        st_ref[...] = jnp.zeros_like(st_ref)

    kg = 8
    n_groups = HEAD_DIM // kg
    p_ref[...] = jnp.ones_like(p_ref)

    def tile(ref, row, a):
        return ref[row, a].reshape(HEAD_DIM, -1)

    def one_step(d, s_ref, d_ref, y_ref, row):
        v_t = tile(s_ref, row, 1)
        p_prev = p_ref[d]
        p_new = p_prev * tile(d_ref, row, 0)
        inv = 1.0 / p_new
        p_ref[d] = p_new
        op_ref[d, 0] = (tile(s_ref, row, 2) * p_prev).reshape(n_groups, kg, -1)
        op_ref[d, 1] = (tile(d_ref, row, 2) * inv).reshape(n_groups, kg, -1)
        op_ref[d, 2] = (tile(d_ref, row, 1) * inv).reshape(n_groups, kg, -1)
        op_ref[d, 3] = (tile(s_ref, row, 0) * p_new).reshape(n_groups, kg, -1)

        sa = jnp.zeros_like(v_t)
        for k in range(HEAD_DIM):
            sa = sa + st_ref[d, k] * op_ref[d, 0, k // kg, pl.ds(k % kg, 1), :]

        def pass_update(g, y):
            for kk in range(kg):
                k = g * kg + kk
                z_new = st_ref[d, k] - sa * op_ref[d, 1, g, pl.ds(kk, 1), :] + v_t * op_ref[d, 2, g, pl.ds(kk, 1), :]
                st_ref[d, k] = z_new
                y = y + z_new * op_ref[d, 3, g, pl.ds(kk, 1), :]
            return y

        y_ref[row] = lax.fori_loop(0, n_groups, pass_update, jnp.zeros_like(v_t))

    def step(jj, carry):
        one_step(0, fs_ref, fd_ref, yf_ref, jj)
        one_step(1, bs_ref, bd_ref, yb_ref, TC_SCAN - 1 - jj)
        return carry

    lax.fori_loop(0, TC_SCAN, step, 0)

    for d in range(2):
        for k in range(HEAD_DIM):
            st_ref[d, k] = st_ref[d, k] * p_ref[d, pl.ds(k, 1), :]


def _scan(feat_t, n_ctx, after):
    s, _, hd, lanes = feat_t.shape
    feat_t = feat_t.reshape(s, N_FEAT, hd // 8, 8, lanes)
    nblk = s // TC_SCAN
    nc = n_ctx // TC_SCAN
    assert n_ctx % TC_SCAN == 0 and s % TC_SCAN == 0

    def bwd(i):
        return jnp.where(i < nc, nc - 1 - i, nblk - 1 - i + nc)

    blk = (TC_SCAN, 3, hd // 8, 8, lanes)
    yblk = (TC_SCAN, hd, lanes)
    return pl.pallas_call(
        _scan_kernel,
        grid=(nblk,),
        in_specs=[pl.BlockSpec(blk, lambda i: (i, 0, 0, 0, 0)),
                  pl.BlockSpec(blk, lambda i: (i, 1, 0, 0, 0)),
                  pl.BlockSpec(blk, lambda i: (bwd(i), 0, 0, 0, 0)),
                  pl.BlockSpec(blk, lambda i: (bwd(i), 2, 0, 0, 0)),
                  pl.BlockSpec((1, 16, after.shape[2]), lambda i: (0, 0, 0))],
        out_specs=[pl.BlockSpec(yblk, lambda i: (i, 0, 0)),
                   pl.BlockSpec(yblk, lambda i: (bwd(i), 0, 0))],
        out_shape=[jax.ShapeDtypeStruct((s, hd, lanes), F32)] * 2,
        scratch_shapes=[pltpu.VMEM((2, HEAD_DIM, hd, lanes), F32),
                        pltpu.VMEM((2, hd, lanes), F32),
                        pltpu.VMEM((2, 4, hd // 8, 8, lanes), F32)],
        compiler_params=_cparams(("arbitrary",)),
        name="rwkv_scan",
    )(feat_t, feat_t, feat_t, feat_t, after)


def _outproj_kernel(yf_ref, yb_ref, rd_ref, yna_ref, x_ref, mod_ref, lnw_ref, lnb_ref, bd_ref, w_ref, o_ref):
    c = D_RWKV
    bd = bd_ref[...]
    y = yf_ref[0] + yb_ref[0]
    mu = _head_sum(y, bd) * (1.0 / HEAD_DIM)
    dlt = y - mu
    var = _head_sum(dlt * dlt, bd) * (1.0 / HEAD_DIM)
    yn = dlt * lax.rsqrt(var + GN_EPS) * lnw_ref[...] + lnb_ref[...]
    y_rw = ((yn + rd_ref[0, :, 0:c]) * rd_ref[0, :, c:2 * c]).astype(BF16)
    o = (jnp.dot(y_rw, w_ref[0:c, :], preferred_element_type=F32)
         + jnp.dot(yna_ref[0], w_ref[c:, :], preferred_element_type=F32))
    o_ref[0] = x_ref[0] + mod_ref[2:3, :] * o


def _outproj(yf, yb, rd, yna, xs, modtab, ln_w, ln_b, bd, w_out_bf, n_ctx_tiles):
    b, s, d = xs.shape
    nj = s // TM
    c = D_RWKV

    def tile(w):
        return pl.BlockSpec((1, TM, w), lambda b, j: (b, j, 0))

    def full(shape):
        return pl.BlockSpec(shape, lambda b, j: (0,) * len(shape))

    return pl.pallas_call(
        _outproj_kernel,
        grid=(b, nj),
        in_specs=[tile(c), tile(c), tile(2 * c), tile(D_NA), tile(d), _mod_spec(n_ctx_tiles),
                  full((1, c)), full((1, c)), full((c, c)), full((d, d))],
        out_specs=tile(d),
        out_shape=jax.ShapeDtypeStruct((b, s, d), F32),
        compiler_params=_cparams(("arbitrary", "arbitrary")),
        name="out_proj",
    )(yf, yb, rd, yna, xs, modtab, ln_w.reshape(1, c), ln_b.reshape(1, c), bd, w_out_bf)


def _ffn_kernel(x_ref, mod_ref, g_ref, w1_ref, w3_ref, w2_ref, o_ref, *, ffc):
    x = x_ref[0]
    h = _norm_mod(x, g_ref[...], mod_ref[4:5, :], mod_ref[3:4, :]).astype(BF16)
    d_ff = w1_ref.shape[1]
    acc = jnp.zeros(x.shape, F32)
    for c0 in range(0, d_ff, ffc):
        a = jnp.dot(h, w1_ref[:, c0:c0 + ffc], preferred_element_type=F32)
        g = jnp.dot(h, w3_ref[:, c0:c0 + ffc], preferred_element_type=F32)
        acc = acc + jnp.dot((_silu(a) * g).astype(BF16), w2_ref[c0:c0 + ffc, :], preferred_element_type=F32)
    o_ref[0] = x + mod_ref[5:6, :] * acc


def _ffn(xs, modtab, g, w1, w3, w2, n_ctx_tiles):
    b, s, d = xs.shape
    nj = s // TM
    d_ff = w1.shape[1]
    const = lambda b, j: (0, 0)
    return pl.pallas_call(
        functools.partial(_ffn_kernel, ffc=d_ff // 2),
        grid=(b, nj),
        in_specs=[pl.BlockSpec((1, TM, d), lambda b, j: (b, j, 0)), _mod_spec(n_ctx_tiles),
                  pl.BlockSpec((1, d), const),
                  pl.BlockSpec((d, d_ff), const), pl.BlockSpec((d, d_ff), const), pl.BlockSpec((d_ff, d), const)],
        out_specs=pl.BlockSpec((1, TM, d), lambda b, j: (b, j, 0)),
        out_shape=jax.ShapeDtypeStruct((b, s, d), F32),
        compiler_params=_cparams(("arbitrary", "arbitrary"), 56),
        name="ffn_swiglu",
    )(xs, modtab, g.reshape(1, d), w1, w3, w2)


def _router_kernel(x_ref, mod_ref, g_ref, r_ref, h_ref, slab_ref):
    h = _norm_mod(x_ref[0], g_ref[...], mod_ref[4:5, :], mod_ref[3:4, :])
    h_ref[0] = h
    logits = jnp.dot(h, r_ref[...], precision=HI, preferred_element_type=F32)
    lane = lax.broadcasted_iota(jnp.int32, logits.shape, 1)
    lg = jnp.where(lane < N_EXPERTS, logits, -jnp.inf)
    m1 = jnp.max(lg, axis=-1, keepdims=True)
    i1 = jnp.min(jnp.where(lg == m1, lane, LANES), axis=-1, keepdims=True)
    lg2 = jnp.where(lane == i1, -jnp.inf, lg)
    m2 = jnp.max(lg2, axis=-1, keepdims=True)
    i2 = jnp.min(jnp.where(lg2 == m2, lane, LANES), axis=-1, keepdims=True)
    e = jnp.exp(m2 - m1)
    g1 = 1.0 / (1.0 + e)
    g2 = e / (1.0 + e)
    slab_ref[0] = jnp.where(lane == 0, i1.astype(F32),
                            jnp.where(lane == 1, i2.astype(F32),
                                      jnp.where(lane == 2, g1, jnp.where(lane == 3, g2, 0.0))))


def _router(xs, modtab, g, router_pad, n_ctx_tiles):
    b, s, d = xs.shape
    nj = s // TM
    const = lambda b, j: (0, 0)
    return pl.pallas_call(
        _router_kernel,
        grid=(b, nj),
        in_specs=[pl.BlockSpec((1, TM, d), lambda b, j: (b, j, 0)), _mod_spec(n_ctx_tiles),
                  pl.BlockSpec((1, d), const), pl.BlockSpec((d, LANES), const)],
        out_specs=[pl.BlockSpec((1, TM, d), lambda b, j: (b, j, 0)),
                   pl.BlockSpec((1, TM, LANES), lambda b, j: (b, j, 0))],
        out_shape=[jax.ShapeDtypeStruct((b, s, d), F32), jax.ShapeDtypeStruct((b, s, LANES), F32)],
        compiler_params=_cparams(("arbitrary", "arbitrary")),
        name="moe_router",
    )(xs, modtab, g.reshape(1, d), router_pad)


def _dispatch(idx):
    n = idx.shape[0]
    a = n * 2
    flat_e = idx.reshape(a)
    onehot = (flat_e[None, :] == jnp.arange(N_EXPERTS, dtype=jnp.int32)[:, None]).astype(jnp.int32)
    csum = jnp.cumsum(onehot, axis=1)
    counts = csum[:, -1]
    rank = jnp.sum(csum * onehot, axis=0) - 1
    padded = (counts + MOE_BLOCK - 1) // MOE_BLOCK * MOE_BLOCK
    pad_ends = jnp.cumsum(padded)
    pad_starts = pad_ends - padded
    dest = (pad_starts[flat_e] + rank).astype(jnp.int32)
    n_blocks = -(-a // MOE_BLOCK) + N_EXPERTS
    slot_tok = jnp.zeros((n_blocks * MOE_BLOCK,), jnp.int32).at[dest].set(jnp.arange(a, dtype=jnp.int32) // 2)
    block_start = jnp.arange(n_blocks, dtype=jnp.int32) * MOE_BLOCK
    block_e = jnp.minimum(jnp.sum((pad_ends[None, :] <= block_start[:, None]).astype(jnp.int32), axis=1),
                          N_EXPERTS - 1)
    n_used = (pad_ends[-1] // MOE_BLOCK).astype(jnp.int32).reshape(1)
    return slot_tok, block_e, n_used, dest


def _moe_mm_kernel(be_ref, st_ref, nu_ref, h_hbm, w1_hbm, w3_hbm, w2_hbm, o_ref, xbuf, w1s, w3s, w2s, stage_cols,
                   stage_rows, gsem, wsem,
                   *, ffc, layer):
    i = pl.program_id(0)
    n_used = nu_ref[0]

    def row_copy(tok, slot, r):
        return pltpu.make_async_copy(h_hbm.at[pl.ds(tok, 1)], xbuf.at[slot, pl.ds(r, 1)], gsem.at[slot])

    def issue(blk, slot):
        def body(r, carry):
            row_copy(st_ref[blk * MOE_BLOCK + r], slot, r).start()
            return carry
        lax.fori_loop(0, MOE_BLOCK, body, 0, unroll=8)

    def wait_rows(slot):
        for r in range(MOE_BLOCK):
            row_copy(0, slot, r).wait()

    n_blocks = pl.num_programs(0)
    slot = lax.rem(i, 2)
    nslot = 1 - slot
    nxt = jnp.minimum(i + 1, n_blocks - 1)

    @pl.when(i == 0)
    def _first():
        issue(0, 0)

    @pl.when(i < n_used)
    def _compute():
        e = be_ref[i]
        changed = (i == 0) | (e != be_ref[jnp.maximum(i - 1, 0)])

        @pl.when(changed)
        def _load_weights():
            nch = w1s.shape[1] // ffc
            chunks = [(m, c) for m in range(3) for c in range(nch)]

            def chunk_copy(n):
                m, c = chunks[n]
                if m < 2:
                    src = (w1_hbm, w3_hbm)[m].at[layer, e, :, pl.ds(c * ffc, ffc)]
                    return pltpu.make_async_copy(src, stage_cols.at[n % 2], wsem.at[n % 2])
                return pltpu.make_async_copy(w2_hbm.at[layer, e, pl.ds(c * ffc, ffc), :], stage_rows.at[n % 2],
                                             wsem.at[n % 2])

            chunk_copy(0).start()
            for n, (m, c) in enumerate(chunks):
                if n + 1 < len(chunks):
                    chunk_copy(n + 1).start()
                chunk_copy(n).wait()
                if m < 2:
                    (w1s, w3s)[m][:, c * ffc:(c + 1) * ffc] = stage_cols[n % 2].astype(BF16)
                else:
                    w2s[c * ffc:(c + 1) * ffc, :] = stage_rows[n % 2].astype(BF16)

        wait_rows(slot)
        xb = xbuf[slot].astype(BF16)
        acc = jnp.zeros(o_ref.shape, F32)
        d_ff = w1s.shape[1]
        n_chunks = d_ff // ffc
        rows_per_chunk = -(-MOE_BLOCK // n_chunks)
        for ci in range(n_chunks):
            c0 = ci * ffc
            a = jnp.dot(xb, w1s[:, c0:c0 + ffc], preferred_element_type=F32)
            g = jnp.dot(xb, w3s[:, c0:c0 + ffc], preferred_element_type=F32)
            acc = acc + jnp.dot((_silu(a) * g).astype(BF16), w2s[c0:c0 + ffc, :], preferred_element_type=F32)
            for r in range(ci * rows_per_chunk, min((ci + 1) * rows_per_chunk, MOE_BLOCK)):
                row_copy(st_ref[nxt * MOE_BLOCK + r], nslot, r).start()
        o_ref[...] = acc

    @pl.when(i >= n_used)
    def _unused():
        o_ref[...] = jnp.zeros_like(o_ref)
        wait_rows(slot)
        issue(nxt, nslot)

    @pl.when(i == n_blocks - 1)
    def _drain():
        wait_rows(nslot)


def _moe_mm(h_flat, slot_tok, block_e, n_used, w1, w3, w2, layer):
    n, d = h_flat.shape
    n_blocks = block_e.shape[0]
    d_ff = w1.shape[3]
    any_spec = pl.BlockSpec(memory_space=pl.ANY)
    return pl.pallas_call(
        functools.partial(_moe_mm_kernel, ffc=MOE_FF_CHUNK, layer=layer),
        grid_spec=pltpu.PrefetchScalarGridSpec(
            num_scalar_prefetch=3,
            grid=(n_blocks,),
            in_specs=[any_spec, any_spec, any_spec, any_spec],
            out_specs=pl.BlockSpec((MOE_BLOCK, d), lambda i, be, st, nu: (i, 0)),
            scratch_shapes=[pltpu.VMEM((2, MOE_BLOCK, d), F32),
                            pltpu.VMEM((d, d_ff), BF16), pltpu.VMEM((d, d_ff), BF16), pltpu.VMEM((d_ff, d), BF16),
                            pltpu.VMEM((2, d, MOE_FF_CHUNK), F32), pltpu.VMEM((2, MOE_FF_CHUNK, d), F32),
                            pltpu.SemaphoreType.DMA((2,)), pltpu.SemaphoreType.DMA((2,))]),
        out_shape=jax.ShapeDtypeStruct((n_blocks * MOE_BLOCK, d), F32),
        compiler_params=_cparams(("arbitrary",), 56),
        name="moe_grouped_swiglu",
    )(block_e, slot_tok, n_used, h_flat, w1, w3, w2)


def _moe_comb_kernel(inv_ref, x_ref, mod_ref, slab_ref, fg_ref, yb_hbm, o_ref, gbuf, gsem, *, nj, final):
    b = pl.program_id(0)
    j = pl.program_id(1)
    t = b * nj + j
    nt = pl.num_programs(0) * nj

    def row_copy(src, slot, which, r):
        return pltpu.make_async_copy(yb_hbm.at[pl.ds(src, 1)], gbuf.at[slot, which, pl.ds(r, 1)], gsem.at[slot])

    def issue(tile, slot):
        def body(r, carry):
            base = (tile * TM + r) * 2
            row_copy(inv_ref[base], slot, 0, r).start()
            row_copy(inv_ref[base + 1], slot, 1, r).start()
            return carry
        lax.fori_loop(0, TM, body, 0, unroll=8)

    def wait_rows(slot):
        for r in range(TM):
            row_copy(0, slot, 0, r).wait()
            row_copy(0, slot, 1, r).wait()

    @pl.when(t == 0)
    def _first():
        issue(0, 0)

    @pl.when(t + 1 < nt)
    def _prefetch():
        issue(t + 1, lax.rem(t + 1, 2))

    slot = lax.rem(t, 2)
    wait_rows(slot)
    slab = slab_ref[0]
    y = slab[:, 2:3] * gbuf[slot, 0] + slab[:, 3:4] * gbuf[slot, 1]
    xn = x_ref[0] + mod_ref[5:6, :] * y
    if final:
        xn = _rms(xn) * fg_ref[...]
    o_ref[0] = xn


def _moe_combine(inv, xs, modtab, slab, final_g, yb, n_ctx_tiles, final):
    b, s, d = xs.shape
    nj = s // TM
    return pl.pallas_call(
        functools.partial(_moe_comb_kernel, nj=nj, final=final),
        grid_spec=pltpu.PrefetchScalarGridSpec(
            num_scalar_prefetch=1,
            grid=(b, nj),
            in_specs=[pl.BlockSpec((1, TM, d), lambda b, j, inv: (b, j, 0)),
                      pl.BlockSpec((None, None, 6, D_MODEL),
                                   lambda b, j, inv: (b, jnp.where(j >= n_ctx_tiles, 1, 0), 0, 0)),
                      pl.BlockSpec((1, TM, LANES), lambda b, j, inv: (b, j, 0)),
                      pl.BlockSpec((1, d), lambda b, j, inv: (0, 0)),
                      pl.BlockSpec(memory_space=pl.ANY)],
            out_specs=pl.BlockSpec((1, TM, d), lambda b, j, inv: (b, j, 0)),
            scratch_shapes=[pltpu.VMEM((2, 2, TM, d), F32), pltpu.SemaphoreType.DMA((2,))]),
        out_shape=jax.ShapeDtypeStruct((b, s, d), F32),
        compiler_params=_cparams(("arbitrary", "arbitrary")),
        name="moe_combine",
    )(inv, xs, modtab, slab, final_g.reshape(1, d), yb)


def _blockdiag(w):
    z = jnp.zeros_like(w[0])
    return jnp.concatenate([jnp.concatenate([w[0], z], axis=1), jnp.concatenate([z, w[1]], axis=1)], axis=0)


def kernel(x, c, ctx, c_ctx, ada_w, ada_b, norm_mix_g, norm_ffn_g, w_in, shift_mu, w0, w2, a0, a2, g2, k_k, k_a, r_k,
           ln_x_w, ln_x_b, na_rpb, w_out, ffn_w1, ffn_w3, ffn_w2, router, moe_w1, moe_w3, moe_w2, final_g):
    b, t, d = x.shape
    n_ctx = ctx.shape[1]
    s = n_ctx + t
    depth = ada_w.shape[0]
    assert d == D_MODEL and n_ctx % TM == 0 and t % TM == 0
    n_ctx_tiles = n_ctx // TM
    lanes = b * H_RWKV

    nb = -(-(b + 1) // 8) * 8
    c_all = jnp.concatenate([c, c_ctx[None, :], jnp.zeros((nb - b - 1, d), F32)], axis=0)
    mods = _mods(c_all, ada_w, ada_b).reshape(depth, nb, 6, d)
    modtab = jnp.stack([jnp.broadcast_to(mods[:, b:b + 1], (depth, b, 6, d)), mods[:, :b]], axis=2)

    head_id = np.arange(D_RWKV) // HEAD_DIM
    bd = jnp.asarray(head_id[:, None] == head_id[None, :], BF16)

    xs = jnp.concatenate([ctx, x], axis=1)
    for l in range(depth):
        last = l == depth - 1
        qkv, u_raw = _inproj(xs, modtab[l], norm_mix_g[l], w_in[l].astype(BF16), n_ctx_tiles)
        feat, rd = _features(u_raw, n_ctx, shift_mu[l], w0[l], _blockdiag(w2[l]), a0[l], _blockdiag(a2[l]), g2[l],
                             k_k[l], k_a[l], r_k[l], bd)
        y_na = _na(qkv, _na_bias_table(na_rpb[l]), n_ctx, rd)
        feat_t = feat.reshape(b, s, N_FEAT, H_RWKV, HEAD_DIM).transpose(1, 2, 4, 0, 3).reshape(
            s, N_FEAT, HEAD_DIM, lanes)
        yf_t, yb_t = _scan(feat_t, n_ctx, y_na)

        def untranspose(y):
            return y.reshape(s, HEAD_DIM, b, H_RWKV).transpose(2, 0, 3, 1).reshape(b, s, D_RWKV)

        xs = _outproj(untranspose(yf_t), untranspose(yb_t), rd, y_na, xs, modtab[l], ln_x_w[l], ln_x_b[l], bd,
                      w_out[l].astype(BF16), n_ctx_tiles)
        i = l // 2
        if l % 2 == 0:
            xs = _ffn(xs, modtab[l], norm_ffn_g[l], ffn_w1[i].astype(BF16), ffn_w3[i].astype(BF16),
                      ffn_w2[i].astype(BF16), n_ctx_tiles)
        else:
            xm, nct = (xs[:, n_ctx:], 0) if last else (xs, n_ctx_tiles)
            n_tok = xm.shape[0] * xm.shape[1]
            router_pad = jnp.pad(router[i], ((0, 0), (0, LANES - N_EXPERTS)))
            h, slab = _router(xm, modtab[l], norm_ffn_g[l], router_pad, nct)
            idx = slab[..., 0:2].astype(jnp.int32).reshape(n_tok, 2)
            slot_tok, block_e, n_used, dest = _dispatch(idx)
            yb = _moe_mm(h.reshape(n_tok, d), slot_tok, block_e, n_used, moe_w1, moe_w3, moe_w2, i)
            xs = _moe_combine(dest, xm, modtab[l], slab, final_g, yb, nct, final=last)
    if depth % 2 == 1:
        raise NotImplementedError("final norm is fused into the MoE combine of the last (odd) layer")
    return xs
```
